```python
import jax, jax.numpy as jnp
from jax import lax
import numpy as np

D_MODEL = 1024
BATCH = 1
SEQ = 16384
DEPTH = 2
DEC_BATCH = 8
DEC_SEQ = 8192
PAST_LEN = 128

D_LRU = D_MODEL
N_LRU_HEADS = 8
LRU_BLOCK = D_LRU // N_LRU_HEADS
CONV_WIDTH = 4
CONV_LEFT = CONV_WIDTH // 2
LRU_C = 8.0
N_DIRS = 2
D_FOURIER = D_MODEL
N_FOURIER_GROUPS = 4
FOURIER_GROUP = D_FOURIER // N_FOURIER_GROUPS
N_BRANCHES = 2
D_IN = 2 * D_LRU + 2 * D_FOURIER + N_BRANCHES * D_MODEL
EPS = 1e-6

kernel_name = "hawk_fnet_parallel_adaln_encoder"


def rms_norm(x, g):
    xf = x.astype(jnp.float32)
    y = xf * lax.rsqrt(jnp.mean(xf * xf, axis=-1, keepdims=True) + EPS)
    return (y * g.astype(jnp.float32)).astype(x.dtype)


def centred_dwconv(x, w, b):
    S = x.shape[1]
    xp = jnp.pad(x, ((0, 0), (CONV_LEFT, CONV_WIDTH - 1 - CONV_LEFT), (0, 0)))
    y = xp[:, 0:S] * w[0] + b
    for k in range(1, CONV_WIDTH):
        y = y + xp[:, k:k + S] * w[k]
    return y


def _lin_combine(left, right):
    a1, b1 = left
    a2, b2 = right
    return a1 * a2, a2 * b1 + b2


def rg_lru(x, w_gates, b_gates, lam, reverse):
    Bn, S, _ = x.shape
    xf = x.astype(jnp.float32)
    xh = xf.reshape(Bn, S, N_LRU_HEADS, LRU_BLOCK)
    g = jnp.einsum('bshi,ghij->gbshj', xh, w_gates.astype(jnp.float32)).reshape(2, Bn, S, D_LRU)
    g = g + b_gates.astype(jnp.float32)[:, None, None, :]
    r = jax.nn.sigmoid(g[0])
    i = jax.nn.sigmoid(g[1])
    log_a = -LRU_C * r * jax.nn.softplus(-lam.astype(jnp.float32))
    a = jnp.exp(log_a)
    mult = jnp.sqrt(-jnp.expm1(2.0 * log_a))
    start = (S - 1) if reverse else 0
    is_start = (jnp.arange(S) == start)[None, :, None]
    mult = jnp.where(is_start, 1.0, mult)
    _, h = lax.associative_scan(_lin_combine, (a, mult * i * xf), reverse=reverse, axis=1)
    return h


def fourier_mix(x):
    Bn, S, _ = x.shape
    xg = x.astype(jnp.float32).reshape(Bn, S, N_FOURIER_GROUPS, FOURIER_GROUP)
    y = jnp.fft.fftn(xg, axes=(1, 3), norm="ortho").real
    return y.reshape(Bn, S, D_FOURIER)


def layer(x, c, norm_g, w_ada, b_ada, w_in, conv_w, conv_b, w_rg, b_rg, lam, w_a_out, w_b_out, w_o):
    mod = jax.nn.silu(c) @ w_ada + b_ada
    shift, scale, gate = jnp.split(mod, 3, axis=-1)
    h = rms_norm(x, norm_g) * (1.0 + scale[:, None, :]) + shift[:, None, :]
    proj = h @ w_in
    xa, ga, xb, gb, merge = jnp.split(
        proj, [D_LRU, 2 * D_LRU, 2 * D_LRU + D_FOURIER, 2 * D_LRU + 2 * D_FOURIER], axis=-1)
    xa = centred_dwconv(xa, conv_w, conv_b)
    ya = rg_lru(xa, w_rg[0], b_rg[0], lam[0], False) + rg_lru(xa, w_rg[1], b_rg[1], lam[1], True)
    ya = (ya.astype(x.dtype) * jax.nn.silu(ga)) @ w_a_out
    yb = (fourier_mix(xb).astype(x.dtype) * jax.nn.silu(gb)) @ w_b_out
    sa, sb = jnp.split(jax.nn.sigmoid(merge), 2, axis=-1)
    out = (sa * ya + sb * yb) @ w_o
    return x + gate[:, None, :] * out


def trunk(x, c, norm_g, w_ada, b_ada, w_in, conv_w, conv_b, w_rg, b_rg, lam, w_a_out, w_b_out, w_o, final_g):
    for l in range(DEPTH):
        x = layer(x, c, norm_g[l], w_ada[l], b_ada[l], w_in[l], conv_w[l], conv_b[l],
                  w_rg[l], b_rg[l], lam[l], w_a_out[l], w_b_out[l], w_o[l])
    return rms_norm(x, final_g)


def setup_inputs(seed: int = 0) -> dict:
    key = jax.random.key(seed)
    ks = jax.random.split(key, 20)
    f32 = jnp.float32
    x_prompt = jax.random.normal(ks[0], (BATCH, SEQ, D_MODEL), f32)
    x_sample = jax.random.normal(ks[1], (DEC_BATCH, DEC_SEQ, D_MODEL), f32)
    c_prompt = jax.random.normal(ks[2], (BATCH, D_MODEL), f32)
    c_sample = jax.random.normal(ks[3], (DEC_BATCH, D_MODEL), f32)
    norm_g = 1.0 + 0.05 * jax.random.normal(ks[4], (DEPTH, D_MODEL), f32)
    w_ada = jax.random.normal(ks[5], (DEPTH, D_MODEL, 3 * D_MODEL), f32) * (0.5 * D_MODEL ** -0.5)
    b_ada = 0.02 * jax.random.normal(ks[6], (DEPTH, 3 * D_MODEL), f32)
    w_in = jax.random.normal(ks[7], (DEPTH, D_MODEL, D_IN), f32) * D_MODEL ** -0.5
    conv_w = jax.random.normal(ks[8], (DEPTH, CONV_WIDTH, D_LRU), f32) * CONV_WIDTH ** -0.5
    conv_b = 0.02 * jax.random.normal(ks[9], (DEPTH, D_LRU), f32)
    w_rg = jax.random.normal(ks[10], (DEPTH, N_DIRS, 2, N_LRU_HEADS, LRU_BLOCK, LRU_BLOCK), f32) * LRU_BLOCK ** -0.5
    b_rg = 0.02 * jax.random.normal(ks[11], (DEPTH, N_DIRS, 2, D_LRU), f32)
    u = jax.random.uniform(ks[12], (DEPTH, N_DIRS, D_LRU), f32, minval=0.9, maxval=0.999)
    a0 = u ** (1.0 / LRU_C)
    lam = jnp.log(a0) - jnp.log1p(-a0)
    w_a_out = jax.random.normal(ks[13], (DEPTH, D_LRU, D_MODEL), f32) * D_LRU ** -0.5
    w_b_out = jax.random.normal(ks[14], (DEPTH, D_FOURIER, D_MODEL), f32) * D_FOURIER ** -0.5
    w_o = jax.random.normal(ks[15], (DEPTH, D_MODEL, D_MODEL), f32) * D_MODEL ** -0.5
    final_g = 1.0 + 0.05 * jax.random.normal(ks[16], (D_MODEL,), f32)
    return {"x_prompt": x_prompt, "x_sample": x_sample, "c_prompt": c_prompt, "c_sample": c_sample,
            "norm_g": norm_g, "w_ada": w_ada, "b_ada": b_ada, "w_in": w_in,
            "conv_w": conv_w, "conv_b": conv_b, "w_rg": w_rg, "b_rg": b_rg, "lam": lam,
            "w_a_out": w_a_out, "w_b_out": w_b_out, "w_o": w_o, "final_g": final_g}


def reference(x_prompt, x_sample, c_prompt, c_sample, norm_g, w_ada, b_ada, w_in, conv_w, conv_b,
              w_rg, b_rg, lam, w_a_out, w_b_out, w_o, final_g):
    y_prompt = trunk(x_prompt, c_prompt, norm_g, w_ada, b_ada, w_in, conv_w, conv_b, w_rg, b_rg, lam,
                     w_a_out, w_b_out, w_o, final_g)
    y_sample = trunk(x_sample, c_sample, norm_g, w_ada, b_ada, w_in, conv_w, conv_b, w_rg, b_rg, lam,
                     w_a_out, w_b_out, w_o, final_g)
    return (y_prompt, y_sample)
```

```python
import functools

import numpy as np
import jax
import jax.numpy as jnp
from jax import lax
from jax.experimental import pallas as pl
from jax.experimental.pallas import tpu as pltpu

F32 = jnp.float32
BF16 = jnp.bfloat16

N_LRU_HEADS = 8
N_FOURIER_GROUPS = 4
CONV_WIDTH = 4
CONV_LEFT = 2
LRU_C = 8.0
EPS = 1e-6

SUBLANES = 8
DFT_N2 = 128
HALO = SUBLANES
VMEM_LIMIT = 56 * 1024 * 1024


def _cparams(n_axes):
    return pltpu.CompilerParams(dimension_semantics=("arbitrary",) * n_axes,
                                vmem_limit_bytes=VMEM_LIMIT)


def _sigmoid(z):
    return jax.nn.sigmoid(z)


def _silu(z):
    return z * _sigmoid(z)


def _softplus(z):
    return jnp.maximum(z, 0.0) + jnp.log1p(jnp.exp(-jnp.abs(z)))


def _norm_mod(x, g1s, shift):
    ms = jnp.mean(x * x, axis=-1, keepdims=True)
    return x * lax.rsqrt(ms + EPS) * g1s + shift


def _mod_rows(mod_ref, d):
    return mod_ref[:, 0:d], mod_ref[:, d:2 * d], mod_ref[:, 2 * d:3 * d]


def _mod_kernel(c_ref, w_ref, b_ref, o_ref):
    c = c_ref[...]
    o_ref[...] = jnp.dot(_silu(c), w_ref[...], preferred_element_type=F32,
                         precision=lax.Precision.HIGHEST) + b_ref[...]


def _adaln_mod(c_all, w_ada, b_ada):
    depth, d, d3 = w_ada.shape
    rows = c_all.shape[0]
    return pl.pallas_call(
        _mod_kernel,
        out_shape=jax.ShapeDtypeStruct((depth, rows, d3), F32),
        grid=(depth, d3 // d),
        in_specs=[pl.BlockSpec((rows, d), lambda l, j: (0, 0)),
                  pl.BlockSpec((None, d, d), lambda l, j: (l, 0, j)),
                  pl.BlockSpec((None, 1, d), lambda l, j: (l, 0, j))],
        out_specs=pl.BlockSpec((None, rows, d), lambda l, j: (l, 0, j)),
        compiler_params=_cparams(2),
        name="adaln_mod",
    )(c_all, w_ada, b_ada.reshape(depth, 1, d3))


def _lru_coeffs(xc_h, g, bg_r, bg_i, k_h, reset_row):
    hb = xc_h.shape[1]
    r = _sigmoid(g[:, :hb] + bg_r)
    ig = _sigmoid(g[:, hb:] + bg_i)
    log_a = k_h * r
    a = jnp.exp(log_a)
    mult = jnp.sqrt(1.0 - a * a)
    row = lax.broadcasted_iota(jnp.int32, a.shape, 0)
    mult = jnp.where(row == reset_row, 1.0, mult)
    return a, mult * ig * xc_h


def _group_scan(a, b, reverse):
    t = a.shape[0]
    pos = lax.broadcasted_iota(jnp.int32, a.shape, 0) % SUBLANES
    d = 1
    while d < SUBLANES:
        if reverse:
            a_s = pltpu.roll(a, t - d, axis=0)
            b_s = pltpu.roll(b, t - d, axis=0)
            m = pos < SUBLANES - d
        else:
            a_s = pltpu.roll(a, d, axis=0)
            b_s = pltpu.roll(b, d, axis=0)
            m = pos >= d
        b = jnp.where(m, a * b_s + b, b)
        a = jnp.where(m, a * a_s, a)
        d *= 2
    return a, b


def _carry_scan(a_ref, b_ref, h_ref, carry, reverse):
    n_groups = a_ref.shape[0] // SUBLANES

    def body(s, c):
        g = (n_groups - 1 - s) if reverse else s
        r0 = pl.multiple_of(g * SUBLANES, SUBLANES)
        h = a_ref[pl.ds(r0, SUBLANES), :] * c + b_ref[pl.ds(r0, SUBLANES), :]
        h_ref[pl.ds(r0, SUBLANES), :] = h
        return h[0:1, :] if reverse else h[SUBLANES - 1:SUBLANES, :]

    return lax.fori_loop(0, n_groups, body, carry, unroll=4)


def _lru_scan_tile(xc, wg_ref, bg_ref, lam_ref, a_ref, b_ref, h_ref, carry_ref, reset_row, reverse):
    d = xc.shape[1]
    hb = d // N_LRU_HEADS
    k = -LRU_C * _softplus(-lam_ref[...])
    xcb = xc.astype(BF16)
    for h in range(N_LRU_HEADS):
        sl = slice(h * hb, (h + 1) * hb)
        g = jnp.dot(xcb[:, sl], wg_ref[h], preferred_element_type=F32)
        a, b = _lru_coeffs(xc[:, sl], g, bg_ref[0:1, sl], bg_ref[1:2, sl], k[:, sl], reset_row)
        a, b = _group_scan(a, b, reverse)
        a_ref[:, sl] = a
        b_ref[:, sl] = b
    carry_ref[...] = _carry_scan(a_ref, b_ref, h_ref, carry_ref[...], reverse)


def _lru_fwd_kernel(x_ref, xp_ref, xn_ref, mod_ref, g_ref, wxa_ref, cw_ref, cb_ref, wg_ref, bg_ref,
                    lam_ref, xc_ref, hf_ref, ext_ref, a_ref, b_ref, carry_ref, *, n_tiles):
    i = pl.program_id(1)
    t, d = x_ref.shape
    shift, scale, _ = _mod_rows(mod_ref, d)
    g1s = g_ref[...] * (1.0 + scale)

    ext_ref[0:HALO, :] = _norm_mod(xp_ref[...], g1s, shift)
    ext_ref[HALO:HALO + t, :] = _norm_mod(x_ref[...], g1s, shift)
    ext_ref[HALO + t:, :] = _norm_mod(xn_ref[...], g1s, shift)
    ext_ref[...] = jnp.dot(ext_ref[...].astype(BF16), wxa_ref[...], preferred_element_type=F32)

    @pl.when(i == 0)
    def _():
        ext_ref[0:HALO, :] = jnp.zeros((HALO, d), F32)
        carry_ref[...] = jnp.zeros_like(carry_ref)

    @pl.when(i == n_tiles - 1)
    def _():
        ext_ref[HALO + t:, :] = jnp.zeros((HALO, d), F32)

    xc = ext_ref[HALO - CONV_LEFT:HALO - CONV_LEFT + t, :] * cw_ref[0:1, :] + cb_ref[...]
    for kk in range(1, CONV_WIDTH):
        o = HALO - CONV_LEFT + kk
        xc = xc + ext_ref[o:o + t, :] * cw_ref[kk:kk + 1, :]
    xc_ref[...] = xc

    reset_row = jnp.where(i == 0, 0, -1)
    _lru_scan_tile(xc, wg_ref, bg_ref, lam_ref, a_ref, b_ref, hf_ref, carry_ref, reset_row, False)


def _lru_fwd(x, mod, norm_g, w_xa, conv_w, conv_b, wg, bg, lam, *, tile):
    bsz, s, d = x.shape
    n_tiles = s // tile
    hpt = tile // HALO
    n_halo = s // HALO
    hb = d // N_LRU_HEADS
    const = lambda b, i: (0, 0)
    return pl.pallas_call(
        functools.partial(_lru_fwd_kernel, n_tiles=n_tiles),
        out_shape=(jax.ShapeDtypeStruct((bsz, s, d), F32), jax.ShapeDtypeStruct((bsz, s, d), F32)),
        grid=(bsz, n_tiles),
        in_specs=[pl.BlockSpec((None, tile, d), lambda b, i: (b, i, 0)),
                  pl.BlockSpec((None, HALO, d), lambda b, i: (b, jnp.maximum(i * hpt - 1, 0), 0)),
                  pl.BlockSpec((None, HALO, d), lambda b, i: (b, jnp.minimum((i + 1) * hpt, n_halo - 1), 0)),
                  pl.BlockSpec((None, 1, 3 * d), lambda b, i: (b, 0, 0)),
                  pl.BlockSpec((1, d), const),
                  pl.BlockSpec((d, d), const),
                  pl.BlockSpec((CONV_WIDTH, d), const),
                  pl.BlockSpec((1, d), const),
                  pl.BlockSpec((N_LRU_HEADS, hb, 2 * hb), lambda b, i: (0, 0, 0)),
                  pl.BlockSpec((2, d), const),
                  pl.BlockSpec((1, d), const)],
        out_specs=(pl.BlockSpec((None, tile, d), lambda b, i: (b, i, 0)),
                   pl.BlockSpec((None, tile, d), lambda b, i: (b, i, 0))),
        scratch_shapes=[pltpu.VMEM((tile + 2 * HALO, d), F32),
                        pltpu.VMEM((tile, d), F32),
                        pltpu.VMEM((tile, d), F32),
                        pltpu.VMEM((1, d), F32)],
        compiler_params=_cparams(2),
        name="lru_fwd",
    )(x, x, x, mod, norm_g, w_xa, conv_w, conv_b, wg, bg, lam)


def _lru_bwd_kernel(x_ref, xc_ref, hf_ref, mod_ref, g_ref, wga_ref, wg_ref, bg_ref, lam_ref, wao_ref,
                    ya_ref, a_ref, b_ref, hb_ref, carry_ref):
    i = pl.program_id(1)
    t, d = x_ref.shape
    shift, scale, _ = _mod_rows(mod_ref, d)
    g1s = g_ref[...] * (1.0 + scale)

    @pl.when(i == 0)
    def _():
        carry_ref[...] = jnp.zeros_like(carry_ref)

    reset_row = jnp.where(i == 0, t - 1, -1)
    _lru_scan_tile(xc_ref[...], wg_ref, bg_ref, lam_ref, a_ref, b_ref, hb_ref, carry_ref, reset_row, True)

    hn = _norm_mod(x_ref[...], g1s, shift).astype(BF16)
    ga = jnp.dot(hn, wga_ref[...], preferred_element_type=F32)
    ya = (hf_ref[...] + hb_ref[...]) * _silu(ga)
    ya_ref[...] = jnp.dot(ya.astype(BF16), wao_ref[...], preferred_element_type=F32)


def _lru_bwd(x, xc, hf, mod, norm_g, w_ga, wg, bg, lam, w_a_out, *, tile):
    bsz, s, d = x.shape
    n_tiles = s // tile
    hb = d // N_LRU_HEADS
    const = lambda b, i: (0, 0)
    rev = lambda b, i: (b, n_tiles - 1 - i, 0)
    return pl.pallas_call(
        _lru_bwd_kernel,
        out_shape=jax.ShapeDtypeStruct((bsz, s, d), F32),
        grid=(bsz, n_tiles),
        in_specs=[pl.BlockSpec((None, tile, d), rev),
                  pl.BlockSpec((None, tile, d), rev),
                  pl.BlockSpec((None, tile, d), rev),
                  pl.BlockSpec((None, 1, 3 * d), lambda b, i: (b, 0, 0)),
                  pl.BlockSpec((1, d), const),
                  pl.BlockSpec((d, d), const),
                  pl.BlockSpec((N_LRU_HEADS, hb, 2 * hb), lambda b, i: (0, 0, 0)),
                  pl.BlockSpec((2, d), const),
                  pl.BlockSpec((1, d), const),
                  pl.BlockSpec((d, d), const)],
        out_specs=pl.BlockSpec((None, tile, d), rev),
        scratch_shapes=[pltpu.VMEM((tile, d), F32),
                        pltpu.VMEM((tile, d), F32),
                        pltpu.VMEM((tile, d), F32),
                        pltpu.VMEM((1, d), F32)],
        compiler_params=_cparams(2),
        name="lru_bwd",
    )(x, xc, hf, mod, norm_g, w_ga, wg, bg, lam, w_a_out)


def _fft_s1_kernel(x_ref, mod_ref, g_ref, wxb_ref, fs1_ref, tw_ref, o_ref, *, tn2):
    n1 = x_ref.shape[0]
    d = x_ref.shape[1] // tn2
    shift, scale, _ = _mod_rows(mod_ref, d)
    g1s = g_ref[...] * (1.0 + scale)
    for j in range(tn2):
        hn = _norm_mod(x_ref[:, j * d:(j + 1) * d], g1s, shift).astype(BF16)
        xb = jnp.dot(hn, wxb_ref[...], preferred_element_type=F32).astype(BF16)
        pq = jnp.dot(fs1_ref[...], xb, preferred_element_type=F32)
        p, q = pq[:n1], pq[n1:]
        tc = tw_ref[:, j:j + 1]
        ts = tw_ref[:, tn2 + j:tn2 + j + 1]
        o_ref[0, j] = (p * tc - q * ts).astype(BF16)
        o_ref[1, j] = (p * ts + q * tc).astype(BF16)


def _fft_s1(x, mod, norm_g, w_xb, fs1, tw, *, tn2):
    bsz, s, d = x.shape
    n1 = s // DFT_N2
    x3 = x.reshape(bsz, n1, DFT_N2 * d)
    const = lambda b, i: (0, 0)
    return pl.pallas_call(
        functools.partial(_fft_s1_kernel, tn2=tn2),
        out_shape=jax.ShapeDtypeStruct((bsz, 2, DFT_N2, n1, d), BF16),
        grid=(bsz, DFT_N2 // tn2),
        in_specs=[
            pl.BlockSpec((None, n1, tn2 * d), lambda b, i: (b, 0, i)),
            pl.BlockSpec((None, 1, 3 * d), lambda b, i: (b, 0, 0)),
            pl.BlockSpec((1, d), const),
            pl.BlockSpec((d, d), const),
            pl.BlockSpec((2 * n1, n1), const),
            pl.BlockSpec((None, n1, 2 * tn2), lambda b, i: (i, 0, 0))],
        out_specs=pl.BlockSpec((None, 2, tn2, n1, d), lambda b, i: (b, 0, i, 0, 0)),
        compiler_params=_cparams(2),
        name="fft_s1",
    )(x3, mod, norm_g, w_xb, fs1, tw)


def _fft_s2_kernel(a_ref, x_ref, ya_ref, mod_ref, g_ref, wgm_ref, m2_ref, cc_ref, cs_ref, wbo_ref, wo_ref,
                   fg_ref, o_ref, *, tk1, final):
    n2 = x_ref.shape[0]
    d = x_ref.shape[1] // tk1
    gw = d // N_FOURIER_GROUPS
    shift, scale, gate = _mod_rows(mod_ref, d)
    g1s = g_ref[...] * (1.0 + scale)

    rhs = a_ref[...].reshape(2 * n2, tk1 * d)
    u = jnp.dot(m2_ref[...], rhs, preferred_element_type=F32)
    for q in range(tk1):
        cs = slice(q * d, (q + 1) * d)
        ur = u[:n2, cs].astype(BF16)
        uq = u[n2:, cs].astype(BF16)
        x = x_ref[:, cs]
        hn = _norm_mod(x, g1s, shift).astype(BF16)
        gm = jnp.dot(hn, wgm_ref[...], preferred_element_type=F32)
        ys = []
        for gi in range(N_FOURIER_GROUPS):
            gs = slice(gi * gw, (gi + 1) * gw)
            ys.append(jnp.dot(ur[:, gs], cc_ref[...], preferred_element_type=F32)
                      + jnp.dot(uq[:, gs], cs_ref[...], preferred_element_type=F32))
        y = jnp.concatenate(ys, axis=1)
        yb = jnp.dot((y * _silu(gm[:, :d])).astype(BF16), wbo_ref[...], preferred_element_type=F32)
        mix = _sigmoid(gm[:, d:2 * d]) * ya_ref[:, cs] + _sigmoid(gm[:, 2 * d:]) * yb
        out = jnp.dot(mix.astype(BF16), wo_ref[...], preferred_element_type=F32)
        xn = x + gate * out
        if final:
            ms = jnp.mean(xn * xn, axis=-1, keepdims=True)
            xn = xn * lax.rsqrt(ms + EPS) * fg_ref[...]
        o_ref[:, cs] = xn


def _fft_s2(a, x, ya, mod, norm_g, w_gm, m2, cc, cs, w_b_out, w_o, final_g, *, tk1, final):
    bsz, s, d = x.shape
    n1 = s // DFT_N2
    a4 = a.reshape(bsz, 2, DFT_N2, n1 * d)
    x3 = x.reshape(bsz, DFT_N2, n1 * d)
    ya3 = ya.reshape(bsz, DFT_N2, n1 * d)
    const = lambda b, i: (0, 0)
    tok = pl.BlockSpec((None, DFT_N2, tk1 * d), lambda b, i: (b, 0, i))
    gw = d // N_FOURIER_GROUPS
    out = pl.pallas_call(
        functools.partial(_fft_s2_kernel, tk1=tk1, final=final),
        out_shape=jax.ShapeDtypeStruct((bsz, DFT_N2, n1 * d), F32),
        grid=(bsz, n1 // tk1),
        in_specs=[pl.BlockSpec((None, 2, DFT_N2, tk1 * d), lambda b, i: (b, 0, 0, i)),
                  tok, tok,
                  pl.BlockSpec((None, 1, 3 * d), lambda b, i: (b, 0, 0)),
                  pl.BlockSpec((1, d), const),
                  pl.BlockSpec((d, 3 * d), const),
                  pl.BlockSpec((2 * DFT_N2, 2 * DFT_N2), const),
                  pl.BlockSpec((gw, gw), const),
                  pl.BlockSpec((gw, gw), const),
                  pl.BlockSpec((d, d), const),
                  pl.BlockSpec((d, d), const),
                  pl.BlockSpec((1, d), const)],
        out_specs=tok,
        compiler_params=_cparams(2),
        name="fft_s2",
    )(a4, x3, ya3, mod, norm_g, w_gm, m2, cc, cs, w_b_out, w_o, final_g)
    return out.reshape(bsz, s, d)


def _dft_tables(s, tn2, gw):
    n1 = s // DFT_N2
    k1 = np.arange(n1)
    ang1 = 2.0 * np.pi * np.outer(k1, k1) / n1
    fs1 = np.concatenate([np.cos(ang1), np.sin(ang1)], axis=0)
    n2 = np.arange(DFT_N2)
    ang_t = 2.0 * np.pi * np.outer(k1, n2) / s
    scale = 1.0 / np.sqrt(float(s) * gw)
    tc = (np.cos(ang_t) * scale).reshape(n1, DFT_N2 // tn2, tn2).transpose(1, 0, 2)
    ts = (np.sin(ang_t) * scale).reshape(n1, DFT_N2 // tn2, tn2).transpose(1, 0, 2)
    tw = np.concatenate([tc, ts], axis=2)
    ang2 = 2.0 * np.pi * np.outer(n2, n2) / DFT_N2
    c2, s2 = np.cos(ang2), np.sin(ang2)
    m2 = np.block([[c2, -s2], [s2, c2]])
    c = np.arange(gw)
    angc = 2.0 * np.pi * np.outer(c, c) / gw
    return (jnp.asarray(fs1, BF16), jnp.asarray(tw, F32), jnp.asarray(m2, BF16),
            jnp.asarray(np.cos(angc), BF16), jnp.asarray(-np.sin(angc), BF16))


def _trunk(x, mods, lw, final_g, *, tile, tn2, tk1):
    s, d = x.shape[1], x.shape[2]
    gw = d // N_FOURIER_GROUPS
    fs1, tw, m2, cc, cs = _dft_tables(s, tn2, gw)
    depth = len(lw)
    for l in range(depth):
        w = lw[l]
        mod = mods[l]
        xc, hf = _lru_fwd(x, mod, w["norm_g"], w["w_xa"], w["conv_w"], w["conv_b"], w["wg"][0], w["bg"][0],
                          w["lam"][0], tile=tile)
        ya = _lru_bwd(x, xc, hf, mod, w["norm_g"], w["w_ga"], w["wg"][1], w["bg"][1], w["lam"][1],
                      w["w_a_out"], tile=tile)
        a = _fft_s1(x, mod, w["norm_g"], w["w_xb"], fs1, tw, tn2=tn2)
        x = _fft_s2(a, x, ya, mod, w["norm_g"], w["w_gm"], m2, cc, cs, w["w_b_out"], w["w_o"], final_g,
                    tk1=tk1, final=(l == depth - 1))
    return x


def _layer_weights(norm_g, w_in, conv_w, conv_b, w_rg, b_rg, lam, w_a_out, w_b_out, w_o):
    depth, d = norm_g.shape
    lw = []
    for l in range(depth):
        wi = w_in[l].astype(BF16)
        lw.append(dict(
            norm_g=norm_g[l].reshape(1, d),
            w_xa=wi[:, 0:d], w_ga=wi[:, d:2 * d], w_xb=wi[:, 2 * d:3 * d], w_gm=wi[:, 3 * d:6 * d],
            conv_w=conv_w[l], conv_b=conv_b[l].reshape(1, d),
            wg=[jnp.concatenate([w_rg[l, dr, 0], w_rg[l, dr, 1]], axis=-1).astype(BF16) for dr in range(2)],
            bg=[b_rg[l, dr] for dr in range(2)],
            lam=[lam[l, dr].reshape(1, d) for dr in range(2)],
            w_a_out=w_a_out[l].astype(BF16), w_b_out=w_b_out[l].astype(BF16), w_o=w_o[l].astype(BF16)))
    return lw


def kernel(x_prompt, x_sample, c_prompt, c_sample, norm_g, w_ada, b_ada, w_in, conv_w, conv_b, w_rg, b_rg, lam,
           w_a_out, w_b_out, w_o, final_g):
    d = x_prompt.shape[-1]
    depth = norm_g.shape[0]
    bp, bs = c_prompt.shape[0], c_sample.shape[0]
    rows = -(-(bp + bs) // SUBLANES) * SUBLANES
    c_all = jnp.concatenate([c_prompt, c_sample, jnp.zeros((rows - bp - bs, d), F32)], axis=0)
    mod = _adaln_mod(c_all, w_ada, b_ada)
    mods_p = [mod[l, 0:bp].reshape(bp, 1, 3 * d) for l in range(depth)]
    mods_s = [mod[l, bp:bp + bs].reshape(bs, 1, 3 * d) for l in range(depth)]
    lw = _layer_weights(norm_g, w_in, conv_w, conv_b, w_rg, b_rg, lam, w_a_out, w_b_out, w_o)
    fg = final_g.reshape(1, d)
    y_prompt = _trunk(x_prompt, mods_p, lw, fg, tile=512, tn2=8, tk1=4)
    y_sample = _trunk(x_sample, mods_s, lw, fg, tile=512, tn2=16, tk1=4)
    return (y_prompt, y_sample)
```

```python
import functools

import numpy as np
import jax
import jax.numpy as jnp
from jax import lax
from jax.experimental import pallas as pl
from jax.experimental.pallas import tpu as pltpu

F32 = jnp.float32
BF16 = jnp.bfloat16

N_LRU_HEADS = 8
N_FOURIER_GROUPS = 4
CONV_WIDTH = 4
CONV_LEFT = 2
LRU_C = 8.0
EPS = 1e-6

SUBLANES = 8
TOK = 128
ROW_CHUNK = 256
VMEM_LIMIT = 58 * 1024 * 1024


def _cparams(n_axes):
    return pltpu.CompilerParams(dimension_semantics=("arbitrary",) * n_axes,
                                vmem_limit_bytes=VMEM_LIMIT)


def _resident(shape, index_map):
    return pl.BlockSpec(shape, index_map, pipeline_mode=pl.Buffered(1))


def _sigmoid(z):
    return 0.5 * jnp.tanh(0.5 * z) + 0.5


def _silu(z):
    return z * _sigmoid(z)


def _softplus(z):
    return jnp.maximum(z, 0.0) + jnp.log1p(jnp.exp(-jnp.abs(z)))


def _norm_mod(x, g1s, shift):
    ms = jnp.mean(x * x, axis=-1, keepdims=True)
    return x * lax.rsqrt(ms + EPS) * g1s + shift


def _mod_rows(mod_ref, d):
    return mod_ref[:, 0:d], mod_ref[:, d:2 * d], mod_ref[:, 2 * d:3 * d]


def _norm_to_rows(x_ref, hbf_ref, g1s, shift, n_groups):
    d = g1s.shape[1]
    g = SUBLANES
    for l in range(0, n_groups, 2):
        xx = jnp.concatenate([x_ref[:, l * d:(l + 1) * d], x_ref[:, (l + 1) * d:(l + 2) * d]], axis=0)
        hbf_ref[l * g:(l + 2) * g, :] = _norm_mod(xx, g1s, shift).astype(BF16)


def _mod_kernel(c_ref, w_ref, b_ref, o_ref):
    c = c_ref[...]
    o_ref[...] = jnp.dot(_silu(c), w_ref[...], preferred_element_type=F32,
                         precision=lax.Precision.HIGHEST) + b_ref[...]


def _adaln_mod(c_all, w_ada, b_ada):
    depth, d, d3 = w_ada.shape
    rows = c_all.shape[0]
    return pl.pallas_call(
        _mod_kernel,
        out_shape=jax.ShapeDtypeStruct((depth, rows, d3), F32),
        grid=(depth, d3 // d),
        in_specs=[pl.BlockSpec((rows, d), lambda l, j: (0, 0)),
                  pl.BlockSpec((None, d, d), lambda l, j: (l, 0, j)),
                  pl.BlockSpec((None, 1, d), lambda l, j: (l, 0, j))],
        out_specs=pl.BlockSpec((None, rows, d), lambda l, j: (l, 0, j)),
        compiler_params=_cparams(2),
        name="adaln_mod",
    )(c_all, w_ada, b_ada.reshape(depth, 1, d3))


def _lru_coeffs(xc_h, g, bg_r, bg_i, k_h, reset_row):
    hb = xc_h.shape[1]
    r = _sigmoid(g[:, :hb] + bg_r)
    ig = _sigmoid(g[:, hb:] + bg_i)
    a = jnp.exp(k_h * r)
    mult = jnp.sqrt(1.0 - a * a)
    row = lax.broadcasted_iota(jnp.int32, a.shape, 0)
    mult = jnp.where(row == reset_row, 1.0, mult)
    return a, mult * ig * xc_h


def _segment_scan(a_ref, b_ref, reverse):
    g = SUBLANES
    n_groups = a_ref.shape[0] // g
    d = a_ref.shape[1]

    def body(s, hp):
        l = (n_groups - 1 - s) if reverse else s
        r0 = pl.multiple_of(l * g, g)
        a = a_ref[pl.ds(r0, g), :]
        h = a * hp[0] + b_ref[pl.ds(r0, g), :]
        p = a * hp[1]
        b_ref[pl.ds(r0, g), :] = h
        a_ref[pl.ds(r0, g), :] = p
        return (h, p)

    return lax.fori_loop(0, n_groups, body, (jnp.zeros((g, d), F32), jnp.ones((g, d), F32)), unroll=4)


def _segment_carries(h_end, p_end, carry, reverse):
    rows = [None] * SUBLANES
    c = carry
    for s in (range(SUBLANES - 1, -1, -1) if reverse else range(SUBLANES)):
        rows[s] = c
        c = h_end[s:s + 1, :] + p_end[s:s + 1, :] * c
    return jnp.concatenate(rows, axis=0), c


def _tile_rows(c, n_rows):
    return jnp.concatenate([c] * (n_rows // c.shape[0]), axis=0)


def _lru_fwd_kernel(x_ref, xn_ref, mod_ref, g_ref, wxa_ref, cw_ref, cb_ref, wg_ref, bg_ref, lam_ref,
                    xc_ref, hf_ref, hbf_ref, ext_ref, a_ref, b_ref, carry_ref, prev_ref, *, n_tiles):
    i = pl.program_id(1)
    g = SUBLANES
    d = g_ref.shape[1]
    n_groups = x_ref.shape[1] // d
    n_rows = n_groups * g
    hb = d // N_LRU_HEADS
    shift, scale, _ = _mod_rows(mod_ref, d)
    g1s = g_ref[...] * (1.0 + scale)

    @pl.when(i == 0)
    def _():
        carry_ref[...] = jnp.zeros_like(carry_ref)
        prev_ref[...] = jnp.zeros_like(prev_ref)

    _norm_to_rows(x_ref, hbf_ref, g1s, shift, n_groups)
    xn = jnp.concatenate([xn_ref[...], xn_ref[...]], axis=0)
    hbf_ref[n_rows:n_rows + 2 * g, :] = _norm_mod(xn, g1s, shift).astype(BF16)
    ext_ref[CONV_LEFT * g:, :] = jnp.dot(hbf_ref[...], wxa_ref[...], preferred_element_type=F32)

    row = lax.broadcasted_iota(jnp.int32, (g, d), 0)
    last1 = ext_ref[(n_groups + 1) * g:(n_groups + 2) * g, :]
    last2 = ext_ref[n_groups * g:(n_groups + 1) * g, :]
    first = ext_ref[CONV_LEFT * g:(CONV_LEFT + 1) * g, :]
    nxt = ext_ref[(n_groups + 2) * g:(n_groups + 3) * g, :]
    ext_ref[g:2 * g, :] = jnp.where(row == 0, pltpu.roll(prev_ref[g:2 * g, :], 1, axis=0),
                                    pltpu.roll(last1, 1, axis=0))
    ext_ref[0:g, :] = jnp.where(row == 0, pltpu.roll(prev_ref[0:g, :], 1, axis=0),
                                pltpu.roll(last2, 1, axis=0))
    prev_ref[0:g, :] = last2
    prev_ref[g:2 * g, :] = last1
    nxt_scale = jnp.where(i == n_tiles - 1, 0.0, 1.0)
    ext_ref[(n_groups + 2) * g:(n_groups + 3) * g, :] = jnp.where(
        row == g - 1, pltpu.roll(nxt, g - 1, axis=0) * nxt_scale, pltpu.roll(first, g - 1, axis=0))

    k = -LRU_C * _softplus(-lam_ref[...])
    reset_row = jnp.where(i == 0, 0, -1)
    for h in range(N_LRU_HEADS):
        sl = slice(h * hb, (h + 1) * hb)
        xc_h = ext_ref[0:n_rows, sl] * cw_ref[0:1, sl] + cb_ref[:, sl]
        for kk in range(1, CONV_WIDTH):
            xc_h = xc_h + ext_ref[kk * g:kk * g + n_rows, sl] * cw_ref[kk:kk + 1, sl]
        xcb = xc_h.astype(BF16)
        xc_ref[:, sl] = xcb
        gates = jnp.dot(xcb, wg_ref[h], preferred_element_type=F32)
        a, b = _lru_coeffs(xc_h, gates, bg_ref[0:1, sl], bg_ref[1:2, sl], k[:, sl], reset_row)
        a_ref[:, sl] = a
        b_ref[:, sl] = b

    h_end, p_end = _segment_scan(a_ref, b_ref, False)
    c_in, carry_ref[...] = _segment_carries(h_end, p_end, carry_ref[...], False)
    c_rows = _tile_rows(c_in, ROW_CHUNK)
    for r0 in range(0, n_rows, ROW_CHUNK):
        rs = slice(r0, r0 + ROW_CHUNK)
        hf_ref[rs, :] = (b_ref[rs, :] + a_ref[rs, :] * c_rows).astype(BF16)


def _lru_fwd(x, mod, norm_g, w_xa, conv_w, conv_b, wg, bg, lam):
    bsz, r, ld = x.shape
    d = ld // TOK
    g = SUBLANES
    n_tiles = r // g
    n_rows = TOK * g
    hb = d // N_LRU_HEADS
    const = lambda b, i: (0, 0)
    rowform = jax.ShapeDtypeStruct((bsz, n_tiles, n_rows, d), BF16)
    return pl.pallas_call(
        functools.partial(_lru_fwd_kernel, n_tiles=n_tiles),
        out_shape=(rowform, rowform),
        grid=(bsz, n_tiles),
        in_specs=[pl.BlockSpec((None, g, ld), lambda b, i: (b, i, 0)),
                  pl.BlockSpec((None, g, d), lambda b, i: (b, jnp.minimum(i + 1, n_tiles - 1), 0)),
                  pl.BlockSpec((None, 1, 3 * d), lambda b, i: (b, 0, 0)),
                  _resident((1, d), const),
                  _resident((d, d), const),
                  _resident((CONV_WIDTH, d), const),
                  _resident((1, d), const),
                  _resident((N_LRU_HEADS, hb, 2 * hb), lambda b, i: (0, 0, 0)),
                  _resident((2, d), const),
                  _resident((1, d), const)],
        out_specs=(pl.BlockSpec((None, None, n_rows, d), lambda b, i: (b, i, 0, 0)),
                   pl.BlockSpec((None, None, n_rows, d), lambda b, i: (b, i, 0, 0))),
        scratch_shapes=[pltpu.VMEM((n_rows + 2 * g, d), BF16),
                        pltpu.VMEM((n_rows + (CONV_LEFT + 2) * g, d), F32),
                        pltpu.VMEM((n_rows, d), F32),
                        pltpu.VMEM((n_rows, d), F32),
                        pltpu.VMEM((1, d), F32),
                        pltpu.VMEM((CONV_LEFT * g, d), F32)],
        compiler_params=_cparams(2),
        name="lru_fwd",
    )(x, x, mod, norm_g, w_xa, conv_w, conv_b, wg, bg, lam)


def _lru_bwd_kernel(x_ref, xc_ref, hf_ref, mod_ref, g_ref, wga_ref, wg_ref, bg_ref, lam_ref, wao_ref,
                    ya_ref, hbf_ref, a_ref, b_ref, carry_ref):
    i = pl.program_id(1)
    g = SUBLANES
    d = g_ref.shape[1]
    n_groups = x_ref.shape[1] // d
    n_rows = n_groups * g
    hb = d // N_LRU_HEADS
    shift, scale, _ = _mod_rows(mod_ref, d)
    g1s = g_ref[...] * (1.0 + scale)

    @pl.when(i == 0)
    def _():
        carry_ref[...] = jnp.zeros_like(carry_ref)

    k = -LRU_C * _softplus(-lam_ref[...])
    reset_row = jnp.where(i == 0, n_rows - 1, -1)
    for h in range(N_LRU_HEADS):
        sl = slice(h * hb, (h + 1) * hb)
        xcb = xc_ref[:, sl]
        gates = jnp.dot(xcb, wg_ref[h], preferred_element_type=F32)
        a, b = _lru_coeffs(xcb.astype(F32), gates, bg_ref[0:1, sl], bg_ref[1:2, sl], k[:, sl], reset_row)
        a_ref[:, sl] = a
        b_ref[:, sl] = b

    h_end, p_end = _segment_scan(a_ref, b_ref, True)
    c_in, carry_ref[...] = _segment_carries(h_end, p_end, carry_ref[...], True)
    c_rows = _tile_rows(c_in, ROW_CHUNK)

    _norm_to_rows(x_ref, hbf_ref, g1s, shift, n_groups)
    for r0 in range(0, n_rows, ROW_CHUNK):
        rs = slice(r0, r0 + ROW_CHUNK)
        ga = jnp.dot(hbf_ref[rs, :], wga_ref[...], preferred_element_type=F32)
        hbwd = b_ref[rs, :] + a_ref[rs, :] * c_rows
        ya = (hf_ref[rs, :].astype(F32) + hbwd) * _silu(ga)
        out = jnp.dot(ya.astype(BF16), wao_ref[...], preferred_element_type=F32)
        for gl in range(ROW_CHUNK // g):
            l = r0 // g + gl
            ya_ref[:, l * d:(l + 1) * d] = out[gl * g:(gl + 1) * g, :]


def _lru_bwd(x, xc, hf, mod, norm_g, w_ga, wg, bg, lam, w_a_out):
    bsz, r, ld = x.shape
    d = ld // TOK
    g = SUBLANES
    n_tiles = r // g
    n_rows = TOK * g
    hb = d // N_LRU_HEADS
    const = lambda b, i: (0, 0)
    rev3 = lambda b, i: (b, n_tiles - 1 - i, 0)
    rev4 = lambda b, i: (b, n_tiles - 1 - i, 0, 0)
    return pl.pallas_call(
        _lru_bwd_kernel,
        out_shape=jax.ShapeDtypeStruct((bsz, r, ld), F32),
        grid=(bsz, n_tiles),
        in_specs=[pl.BlockSpec((None, g, ld), rev3),
                  pl.BlockSpec((None, None, n_rows, d), rev4),
                  pl.BlockSpec((None, None, n_rows, d), rev4),
                  pl.BlockSpec((None, 1, 3 * d), lambda b, i: (b, 0, 0)),
                  _resident((1, d), const),
                  _resident((d, d), const),
                  _resident((N_LRU_HEADS, hb, 2 * hb), lambda b, i: (0, 0, 0)),
                  _resident((2, d), const),
                  _resident((1, d), const),
                  _resident((d, d), const)],
        out_specs=pl.BlockSpec((None, g, ld), rev3),
        scratch_shapes=[pltpu.VMEM((n_rows, d), BF16),
                        pltpu.VMEM((n_rows, d), F32),
                        pltpu.VMEM((n_rows, d), F32),
                        pltpu.VMEM((1, d), F32)],
        compiler_params=_cparams(2),
        name="lru_bwd",
    )(x, xc, hf, mod, norm_g, w_ga, wg, bg, lam, w_a_out)


def _fft_s1_kernel(*refs, tn, halves):
    x_refs = refs[:halves]
    tw_refs = refs[halves:2 * halves]
    mod_ref, g_ref, wxb_ref, fs1_ref, o_ref = refs[2 * halves:]
    r = x_refs[0].shape[0]
    d = g_ref.shape[1]
    shift, scale, _ = _mod_rows(mod_ref, d)
    g1s = g_ref[...] * (1.0 + scale)
    for j in range(tn):
        parts = []
        for hf in range(halves):
            hn = _norm_mod(x_refs[hf][:, j * d:(j + 1) * d], g1s, shift).astype(BF16)
            xb = jnp.dot(hn, wxb_ref[...], preferred_element_type=F32).astype(BF16)
            pq = jnp.dot(fs1_ref[...], xb, preferred_element_type=F32)
            p, q = pq[:r], pq[r:]
            tc = tw_refs[hf][:, j:j + 1]
            ts = tw_refs[hf][:, tn + j:tn + j + 1]
            parts.append((p * tc - q * ts, p * ts + q * tc))
        if halves == 1:
            o_ref[0, 0, j] = parts[0][0].astype(BF16)
            o_ref[1, 0, j] = parts[0][1].astype(BF16)
        else:
            for ri in range(2):
                o_ref[ri, 0, j] = (parts[0][ri] + parts[1][ri]).astype(BF16)
                o_ref[ri, 1, j] = (parts[0][ri] - parts[1][ri]).astype(BF16)


def _fft_s1(x, mod, norm_g, w_xb, fs1, tw, *, tn, halves):
    bsz, r, ld = x.shape
    d = ld // TOK
    nc = TOK // halves
    nblk = nc // tn
    const = lambda b, i: (0, 0)
    x_specs = [pl.BlockSpec((None, r, tn * d), functools.partial(lambda b, i, hf: (b, 0, hf * nblk + i), hf=hf))
               for hf in range(halves)]
    tw_specs = [pl.BlockSpec((None, r, 2 * tn), functools.partial(lambda b, i, hf: (hf * nblk + i, 0, 0), hf=hf))
                for hf in range(halves)]
    return pl.pallas_call(
        functools.partial(_fft_s1_kernel, tn=tn, halves=halves),
        out_shape=jax.ShapeDtypeStruct((bsz, 2, halves, nc, r, d), BF16),
        grid=(bsz, nblk),
        in_specs=x_specs + tw_specs + [
            pl.BlockSpec((None, 1, 3 * d), lambda b, i: (b, 0, 0)),
            _resident((1, d), const),
            _resident((d, d), const),
            _resident((2 * r, r), const)],
        out_specs=pl.BlockSpec((None, 2, halves, tn, r, d), lambda b, i: (b, 0, 0, i, 0, 0)),
        compiler_params=_cparams(2),
        name="fft_s1",
    )(*([x] * halves), *([tw] * halves), mod, norm_g, w_xb, fs1)


def _fft_s2_kernel(a_ref, x_ref, ya_ref, mod_ref, g_ref, wgm_ref, m2_ref, cc_ref, cs_ref, wbo_ref, wo_ref,
                   fg_ref, o_ref, *, tk1, final):
    r = x_ref.shape[0]
    d = g_ref.shape[1]
    gw = d // N_FOURIER_GROUPS
    shift, scale, gate = _mod_rows(mod_ref, d)
    g1s = g_ref[...] * (1.0 + scale)
    cols = [slice(q * d, (q + 1) * d) for q in range(tk1)]

    rhs = a_ref[...].reshape(2 * r, tk1 * d)
    u = jnp.dot(m2_ref[...], rhs, preferred_element_type=F32)
    ur = jnp.concatenate([u[:r, cs] for cs in cols], axis=0).astype(BF16)
    uq = jnp.concatenate([u[r:, cs] for cs in cols], axis=0).astype(BF16)
    x = jnp.concatenate([x_ref[:, cs] for cs in cols], axis=0)
    ya = jnp.concatenate([ya_ref[:, cs] for cs in cols], axis=0)

    hn = _norm_mod(x, g1s, shift).astype(BF16)
    gm = jnp.dot(hn, wgm_ref[...], preferred_element_type=F32)
    y = jnp.concatenate(
        [jnp.dot(ur[:, gi * gw:(gi + 1) * gw], cc_ref[...], preferred_element_type=F32)
         + jnp.dot(uq[:, gi * gw:(gi + 1) * gw], cs_ref[...], preferred_element_type=F32)
         for gi in range(N_FOURIER_GROUPS)], axis=1)
    yb = jnp.dot((y * _silu(gm[:, :d])).astype(BF16), wbo_ref[...], preferred_element_type=F32)
    mix = _sigmoid(gm[:, d:2 * d]) * ya + _sigmoid(gm[:, 2 * d:]) * yb
    out = jnp.dot(mix.astype(BF16), wo_ref[...], preferred_element_type=F32)
    xn = x + gate * out
    if final:
        ms = jnp.mean(xn * xn, axis=-1, keepdims=True)
        xn = xn * lax.rsqrt(ms + EPS) * fg_ref[...]
    for q, cs in enumerate(cols):
        o_ref[:, cs] = xn[q * r:(q + 1) * r, :]


def _fft_s2(a, x, ya, mod, norm_g, w_gm, m2, cc, cs, w_b_out, w_o, final_g, *, tk1, final):
    bsz, r, ld = x.shape
    d = ld // TOK
    halves = a.shape[2]
    nc = a.shape[3]
    nblk = (TOK // halves) // tk1
    a5 = a.reshape(bsz, 2, halves, nc, (TOK // halves) * d)
    const = lambda b, k2, i: (0, 0)
    tok = pl.BlockSpec((None, r, tk1 * d), lambda b, k2, i: (b, 0, k2 * nblk + i))
    gw = d // N_FOURIER_GROUPS
    return pl.pallas_call(
        functools.partial(_fft_s2_kernel, tk1=tk1, final=final),
        out_shape=jax.ShapeDtypeStruct((bsz, r, ld), F32),
        grid=(bsz, halves, nblk),
        in_specs=[pl.BlockSpec((None, 2, None, nc, tk1 * d), lambda b, k2, i: (b, 0, k2, 0, i)),
                  tok, tok,
                  pl.BlockSpec((None, 1, 3 * d), lambda b, k2, i: (b, 0, 0)),
                  _resident((1, d), const),
                  _resident((d, 3 * d), const),
                  pl.BlockSpec((None, 2 * nc, 2 * nc), lambda b, k2, i: (k2, 0, 0)),
                  _resident((gw, gw), const),
                  _resident((gw, gw), const),
                  _resident((d, d), const),
                  _resident((d, d), const),
                  _resident((1, d), const)],
        out_specs=tok,
        compiler_params=_cparams(3),
        name="fft_s2",
    )(a5, x, ya, mod, norm_g, w_gm, m2, cc, cs, w_b_out, w_o, final_g)


def _dft_tables(r, tn, gw):
    halves = TOK // r
    s = r * TOK
    k1 = np.arange(r)
    ang1 = 2.0 * np.pi * np.outer(k1, k1) / r
    fs1 = np.concatenate([np.cos(ang1), np.sin(ang1)], axis=0)
    tok = np.arange(TOK)
    ang_t = 2.0 * np.pi * np.outer(k1, tok) / s
    scale = 1.0 / np.sqrt(float(s) * gw)
    tc = (np.cos(ang_t) * scale).reshape(r, TOK // tn, tn).transpose(1, 0, 2)
    ts = (np.sin(ang_t) * scale).reshape(r, TOK // tn, tn).transpose(1, 0, 2)
    tw = np.concatenate([tc, ts], axis=2)
    n = np.arange(r)
    m2 = []
    for k2 in range(halves):
        ang2 = 2.0 * np.pi * np.outer(halves * n + k2, n) / (halves * r)
        c2, s2 = np.cos(ang2), np.sin(ang2)
        m2.append(np.block([[c2, -s2], [s2, c2]]))
    c = np.arange(gw)
    angc = 2.0 * np.pi * np.outer(c, c) / gw
    return (jnp.asarray(fs1, BF16), jnp.asarray(tw, F32), jnp.asarray(np.stack(m2), BF16),
            jnp.asarray(np.cos(angc), BF16), jnp.asarray(-np.sin(angc), BF16))


def _trunk(x, mods, lw, final_g):
    bsz, s, d = x.shape
    r = s // TOK
    halves = TOK // r
    assert r * TOK == s and r * halves == TOK and halves in (1, 2) and r % SUBLANES == 0
    tn = 8
    tk1 = 512 // r
    gw = d // N_FOURIER_GROUPS
    fs1, tw, m2, cc, cs = _dft_tables(r, tn, gw)
    x = x.reshape(bsz, r, TOK * d)
    depth = len(lw)
    for l in range(depth):
        w = lw[l]
        mod = mods[l]
        xc, hf = _lru_fwd(x, mod, w["norm_g"], w["w_xa"], w["conv_w"], w["conv_b"], w["wg"][0], w["bg"][0],
                          w["lam"][0])
        ya = _lru_bwd(x, xc, hf, mod, w["norm_g"], w["w_ga"], w["wg"][1], w["bg"][1], w["lam"][1],
                      w["w_a_out"])
        a = _fft_s1(x, mod, w["norm_g"], w["w_xb"], fs1, tw, tn=tn, halves=halves)
        x = _fft_s2(a, x, ya, mod, w["norm_g"], w["w_gm"], m2, cc, cs, w["w_b_out"], w["w_o"], final_g,
                    tk1=tk1, final=(l == depth - 1))
    return x.reshape(bsz, s, d)


def _layer_weights(norm_g, w_in, conv_w, conv_b, w_rg, b_rg, lam, w_a_out, w_b_out, w_o):
    depth, d = norm_g.shape
    lw = []
    for l in range(depth):
        wi = w_in[l].astype(BF16)
        lw.append(dict(
            norm_g=norm_g[l].reshape(1, d),
            w_xa=wi[:, 0:d], w_ga=wi[:, d:2 * d], w_xb=wi[:, 2 * d:3 * d], w_gm=wi[:, 3 * d:6 * d],
            conv_w=conv_w[l], conv_b=conv_b[l].reshape(1, d),
            wg=[jnp.concatenate([w_rg[l, dr, 0], w_rg[l, dr, 1]], axis=-1).astype(BF16) for dr in range(2)],
            bg=[b_rg[l, dr] for dr in range(2)],
            lam=[lam[l, dr].reshape(1, d) for dr in range(2)],
            w_a_out=w_a_out[l].astype(BF16), w_b_out=w_b_out[l].astype(BF16), w_o=w_o[l].astype(BF16)))
    return lw


def kernel(x_prompt, x_sample, c_prompt, c_sample, norm_g, w_ada, b_ada, w_in, conv_w, conv_b, w_rg, b_rg, lam,
           w_a_out, w_b_out, w_o, final_g):
    d = x_prompt.shape[-1]
    depth = norm_g.shape[0]
    bp, bs = c_prompt.shape[0], c_sample.shape[0]
    rows = -(-(bp + bs) // SUBLANES) * SUBLANES
    c_all = jnp.concatenate([c_prompt, c_sample, jnp.zeros((rows - bp - bs, d), F32)], axis=0)
    mod = _adaln_mod(c_all, w_ada, b_ada)
    mods_p = [mod[l, 0:bp].reshape(bp, 1, 3 * d) for l in range(depth)]
    mods_s = [mod[l, bp:bp + bs].reshape(bs, 1, 3 * d) for l in range(depth)]
    lw = _layer_weights(norm_g, w_in, conv_w, conv_b, w_rg, b_rg, lam, w_a_out, w_b_out, w_o)
    fg = final_g.reshape(1, d)
    return (_trunk(x_prompt, mods_p, lw, fg), _trunk(x_sample, mods_s, lw, fg))
```

```python
import functools

import numpy as np
import jax
import jax.numpy as jnp
from jax import lax
from jax.experimental import pallas as pl
from jax.experimental.pallas import tpu as pltpu

F32 = jnp.float32
BF16 = jnp.bfloat16

N_LRU_HEADS = 8
N_FOURIER_GROUPS = 4
CONV_WIDTH = 4
CONV_LEFT = 2
LRU_C = 8.0
EPS = 1e-6

SUBLANES = 8
TOK = 128
ROW_CHUNK = 256
VMEM_LIMIT = 58 * 1024 * 1024
TINY = 1e-30
LOG2E = 1.4426950408889634


def _cparams(n_axes):
    return pltpu.CompilerParams(dimension_semantics=("arbitrary",) * n_axes,
                                vmem_limit_bytes=VMEM_LIMIT)


def _resident(shape, index_map):
    return pl.BlockSpec(shape, index_map, pipeline_mode=pl.Buffered(1))


def _sigmoid(z):
    return 0.5 * jnp.tanh(0.5 * z) + 0.5


def _silu(z):
    return z * _sigmoid(z)


def _softplus(z):
    return jnp.maximum(z, 0.0) + jnp.log1p(jnp.exp(-jnp.abs(z)))


def _norm_mod(x, g1s, shift):
    ms = jnp.mean(x * x, axis=-1, keepdims=True)
    return x * lax.rsqrt(ms + EPS) * g1s + shift


def _mod_rows(mod_ref, d):
    return mod_ref[:, 0:d], mod_ref[:, d:2 * d], mod_ref[:, 2 * d:3 * d]


def _norm_to_rows(x_ref, hbf_ref, g1s, shift, n_groups):
    d = g1s.shape[1]
    g = SUBLANES
    for l in range(0, n_groups, 2):
        xx = jnp.concatenate([x_ref[:, l * d:(l + 1) * d], x_ref[:, (l + 1) * d:(l + 2) * d]], axis=0)
        hbf_ref[l * g:(l + 2) * g, :] = _norm_mod(xx, g1s, shift).astype(BF16)


def _mod_kernel(c_ref, w_ref, b_ref, o_ref):
    c = c_ref[...]
    o_ref[...] = jnp.dot(_silu(c), w_ref[...], preferred_element_type=F32,
                         precision=lax.Precision.HIGHEST) + b_ref[...]


def _adaln_mod(c_all, w_ada, b_ada):
    depth, d, d3 = w_ada.shape
    rows = c_all.shape[0]
    return pl.pallas_call(
        _mod_kernel,
        out_shape=jax.ShapeDtypeStruct((depth, rows, d3), F32),
        grid=(depth, d3 // d),
        in_specs=[pl.BlockSpec((rows, d), lambda l, j: (0, 0)),
                  pl.BlockSpec((None, d, d), lambda l, j: (l, 0, j)),
                  pl.BlockSpec((None, 1, d), lambda l, j: (l, 0, j))],
        out_specs=pl.BlockSpec((None, rows, d), lambda l, j: (l, 0, j)),
        compiler_params=_cparams(2),
        name="adaln_mod",
    )(c_all, w_ada, b_ada.reshape(depth, 1, d3))


def _lru_coeffs(xh, gh, hbg_r, hbg_i, k2):
    hb = xh.shape[1]
    t_r = jnp.tanh(gh[:, :hb] + hbg_r)
    t_i = jnp.tanh(gh[:, hb:] + hbg_i)
    a = jnp.exp2((t_r + 1.0) * k2)
    y = 1.0 - a * a
    mult = y * lax.rsqrt(jnp.maximum(y, TINY))
    u = (t_i + 1.0) * xh
    return a, mult * u, u


def _gate_consts(bg_ref, lam_ref):
    return 0.5 * bg_ref[...], (-0.5 * LRU_C * LOG2E) * _softplus(-lam_ref[...])


def _segment_scan(a_ref, b_ref, reverse):
    g = SUBLANES
    n_groups = a_ref.shape[0] // g
    d = a_ref.shape[1]

    def body(s, hp):
        l = (n_groups - 1 - s) if reverse else s
        r0 = pl.multiple_of(l * g, g)
        a = a_ref[pl.ds(r0, g), :]
        h = a * hp[0] + b_ref[pl.ds(r0, g), :]
        p = a * hp[1]
        b_ref[pl.ds(r0, g), :] = h
        a_ref[pl.ds(r0, g), :] = p
        return (h, p)

    return lax.fori_loop(0, n_groups, body, (jnp.zeros((g, d), F32), jnp.ones((g, d), F32)), unroll=4)


def _segment_carries(h_end, p_end, carry, reverse):
    rows = [None] * SUBLANES
    c = carry
    for s in (range(SUBLANES - 1, -1, -1) if reverse else range(SUBLANES)):
        rows[s] = c
        c = h_end[s:s + 1, :] + p_end[s:s + 1, :] * c
    return jnp.concatenate(rows, axis=0), c


def _tile_rows(c, n_rows):
    return jnp.concatenate([c] * (n_rows // c.shape[0]), axis=0)


def _lru_fwd_kernel(x_ref, xn_ref, mod_ref, g_ref, wxa_ref, cw_ref, cb_ref, wg_ref, bg_ref, lam_ref,
                    xc_ref, hf_ref, hbf_ref, ext_ref, a_ref, b_ref, carry_ref, prev_ref, u_ref, *, n_tiles):
    i = pl.program_id(1)
    g = SUBLANES
    d = g_ref.shape[1]
    n_groups = x_ref.shape[1] // d
    n_rows = n_groups * g
    hb = d // N_LRU_HEADS
    shift, scale, _ = _mod_rows(mod_ref, d)
    g1s = g_ref[...] * (1.0 + scale)

    @pl.when(i == 0)
    def _():
        carry_ref[...] = jnp.zeros_like(carry_ref)
        prev_ref[...] = jnp.zeros_like(prev_ref)

    _norm_to_rows(x_ref, hbf_ref, g1s, shift, n_groups)
    xn = jnp.concatenate([xn_ref[...], xn_ref[...]], axis=0)
    hbf_ref[n_rows:n_rows + 2 * g, :] = _norm_mod(xn, g1s, shift).astype(BF16)
    ext_ref[CONV_LEFT * g:, :] = jnp.dot(hbf_ref[...], wxa_ref[...], preferred_element_type=F32)

    row = lax.broadcasted_iota(jnp.int32, (g, d), 0)
    last1 = ext_ref[(n_groups + 1) * g:(n_groups + 2) * g, :]
    last2 = ext_ref[n_groups * g:(n_groups + 1) * g, :]
    first = ext_ref[CONV_LEFT * g:(CONV_LEFT + 1) * g, :]
    nxt = ext_ref[(n_groups + 2) * g:(n_groups + 3) * g, :]
    ext_ref[g:2 * g, :] = jnp.where(row == 0, pltpu.roll(prev_ref[g:2 * g, :], 1, axis=0),
                                    pltpu.roll(last1, 1, axis=0))
    ext_ref[0:g, :] = jnp.where(row == 0, pltpu.roll(prev_ref[0:g, :], 1, axis=0),
                                pltpu.roll(last2, 1, axis=0))
    prev_ref[0:g, :] = last2
    prev_ref[g:2 * g, :] = last1
    nxt_scale = jnp.where(i == n_tiles - 1, 0.0, 1.0)
    ext_ref[(n_groups + 2) * g:(n_groups + 3) * g, :] = jnp.where(
        row == g - 1, pltpu.roll(nxt, g - 1, axis=0) * nxt_scale, pltpu.roll(first, g - 1, axis=0))

    hbg, k2 = _gate_consts(bg_ref, lam_ref)
    hcw = 0.5 * cw_ref[...]
    hcb = 0.5 * cb_ref[...]
    for h in range(N_LRU_HEADS):
        sl = slice(h * hb, (h + 1) * hb)
        xh = ext_ref[0:n_rows, sl] * hcw[0:1, sl] + hcb[:, sl]
        for kk in range(1, CONV_WIDTH):
            xh = xh + ext_ref[kk * g:kk * g + n_rows, sl] * hcw[kk:kk + 1, sl]
        xhb = xh.astype(BF16)
        xc_ref[:, sl] = xhb
        gh = jnp.dot(xhb, wg_ref[h], preferred_element_type=F32)
        a, b, u = _lru_coeffs(xh, gh, hbg[0:1, sl], hbg[1:2, sl], k2[:, sl])
        a_ref[:, sl] = a
        b_ref[:, sl] = b
        u_ref[:, sl] = u[0:g, :]

    @pl.when(i == 0)
    def _():
        b_ref[0:g, :] = jnp.where(row == 0, u_ref[...], b_ref[0:g, :])

    h_end, p_end = _segment_scan(a_ref, b_ref, False)
    c_in, carry_ref[...] = _segment_carries(h_end, p_end, carry_ref[...], False)
    c_rows = _tile_rows(c_in, ROW_CHUNK)
    for r0 in range(0, n_rows, ROW_CHUNK):
        rs = slice(r0, r0 + ROW_CHUNK)
        hf_ref[rs, :] = (b_ref[rs, :] + a_ref[rs, :] * c_rows).astype(BF16)


def _lru_fwd(x, mod, norm_g, w_xa, conv_w, conv_b, wg, bg, lam):
    bsz, r, ld = x.shape
    d = ld // TOK
    g = SUBLANES
    n_tiles = r // g
    n_rows = TOK * g
    hb = d // N_LRU_HEADS
    const = lambda b, i: (0, 0)
    rowform = jax.ShapeDtypeStruct((bsz, n_tiles, n_rows, d), BF16)
    return pl.pallas_call(
        functools.partial(_lru_fwd_kernel, n_tiles=n_tiles),
        out_shape=(rowform, rowform),
        grid=(bsz, n_tiles),
        in_specs=[pl.BlockSpec((None, g, ld), lambda b, i: (b, i, 0)),
                  pl.BlockSpec((None, g, d), lambda b, i: (b, jnp.minimum(i + 1, n_tiles - 1), 0)),
                  pl.BlockSpec((None, 1, 3 * d), lambda b, i: (b, 0, 0)),
                  _resident((1, d), const),
                  _resident((d, d), const),
                  _resident((CONV_WIDTH, d), const),
                  _resident((1, d), const),
                  _resident((N_LRU_HEADS, hb, 2 * hb), lambda b, i: (0, 0, 0)),
                  _resident((2, d), const),
                  _resident((1, d), const)],
        out_specs=(pl.BlockSpec((None, None, n_rows, d), lambda b, i: (b, i, 0, 0)),
                   pl.BlockSpec((None, None, n_rows, d), lambda b, i: (b, i, 0, 0))),
        scratch_shapes=[pltpu.VMEM((n_rows + 2 * g, d), BF16),
                        pltpu.VMEM((n_rows + (CONV_LEFT + 2) * g, d), F32),
                        pltpu.VMEM((n_rows, d), F32),
                        pltpu.VMEM((n_rows, d), F32),
                        pltpu.VMEM((1, d), F32),
                        pltpu.VMEM((CONV_LEFT * g, d), F32),
                        pltpu.VMEM((g, d), F32)],
        compiler_params=_cparams(2),
        name="lru_fwd",
    )(x, x, mod, norm_g, w_xa, conv_w, conv_b, wg, bg, lam)


def _lru_bwd_kernel(x_ref, xc_ref, hf_ref, mod_ref, g_ref, wga_ref, wg_ref, bg_ref, lam_ref, wao_ref,
                    ya_ref, hbf_ref, a_ref, b_ref, carry_ref, u_ref):
    i = pl.program_id(1)
    g = SUBLANES
    d = g_ref.shape[1]
    n_groups = x_ref.shape[1] // d
    n_rows = n_groups * g
    hb = d // N_LRU_HEADS
    shift, scale, _ = _mod_rows(mod_ref, d)
    g1s = g_ref[...] * (1.0 + scale)

    @pl.when(i == 0)
    def _():
        carry_ref[...] = jnp.zeros_like(carry_ref)

    hbg, k2 = _gate_consts(bg_ref, lam_ref)
    for h in range(N_LRU_HEADS):
        sl = slice(h * hb, (h + 1) * hb)
        xhb = xc_ref[:, sl]
        gh = jnp.dot(xhb, wg_ref[h], preferred_element_type=F32)
        a, b, u = _lru_coeffs(xhb.astype(F32), gh, hbg[0:1, sl], hbg[1:2, sl], k2[:, sl])
        a_ref[:, sl] = a
        b_ref[:, sl] = b
        u_ref[:, sl] = u[n_rows - g:, :]

    @pl.when(i == 0)
    def _():
        row = lax.broadcasted_iota(jnp.int32, (g, d), 0)
        b_ref[n_rows - g:, :] = jnp.where(row == g - 1, u_ref[...], b_ref[n_rows - g:, :])

    h_end, p_end = _segment_scan(a_ref, b_ref, True)
    c_in, carry_ref[...] = _segment_carries(h_end, p_end, carry_ref[...], True)
    c_rows = _tile_rows(c_in, ROW_CHUNK)

    _norm_to_rows(x_ref, hbf_ref, g1s, shift, n_groups)
    for r0 in range(0, n_rows, ROW_CHUNK):
        rs = slice(r0, r0 + ROW_CHUNK)
        ga = jnp.dot(hbf_ref[rs, :], wga_ref[...], preferred_element_type=F32)
        hbwd = b_ref[rs, :] + a_ref[rs, :] * c_rows
        ya = (hf_ref[rs, :].astype(F32) + hbwd) * _silu(ga)
        out = jnp.dot(ya.astype(BF16), wao_ref[...], preferred_element_type=F32)
        for gl in range(ROW_CHUNK // g):
            l = r0 // g + gl
            ya_ref[:, l * d:(l + 1) * d] = out[gl * g:(gl + 1) * g, :]


def _lru_bwd(x, xc, hf, mod, norm_g, w_ga, wg, bg, lam, w_a_out):
    bsz, r, ld = x.shape
    d = ld // TOK
    g = SUBLANES
    n_tiles = r // g
    n_rows = TOK * g
    hb = d // N_LRU_HEADS
    const = lambda b, i: (0, 0)
    rev3 = lambda b, i: (b, n_tiles - 1 - i, 0)
    rev4 = lambda b, i: (b, n_tiles - 1 - i, 0, 0)
    return pl.pallas_call(
        _lru_bwd_kernel,
        out_shape=jax.ShapeDtypeStruct((bsz, r, ld), F32),
        grid=(bsz, n_tiles),
        in_specs=[pl.BlockSpec((None, g, ld), rev3),
                  pl.BlockSpec((None, None, n_rows, d), rev4),
                  pl.BlockSpec((None, None, n_rows, d), rev4),
                  pl.BlockSpec((None, 1, 3 * d), lambda b, i: (b, 0, 0)),
                  _resident((1, d), const),
                  _resident((d, d), const),
                  _resident((N_LRU_HEADS, hb, 2 * hb), lambda b, i: (0, 0, 0)),
                  _resident((2, d), const),
                  _resident((1, d), const),
                  _resident((d, d), const)],
        out_specs=pl.BlockSpec((None, g, ld), rev3),
        scratch_shapes=[pltpu.VMEM((n_rows, d), BF16),
                        pltpu.VMEM((n_rows, d), F32),
                        pltpu.VMEM((n_rows, d), F32),
                        pltpu.VMEM((1, d), F32),
                        pltpu.VMEM((g, d), F32)],
        compiler_params=_cparams(2),
        name="lru_bwd",
    )(x, xc, hf, mod, norm_g, w_ga, wg, bg, lam, w_a_out)


def _fft_s1_kernel(*refs, tn, halves):
    x_refs = refs[:halves]
    tw_refs = refs[halves:2 * halves]
    mod_ref, g_ref, wxb_ref, fs1_ref, o_ref, hbf_ref = refs[2 * halves:]
    r = x_refs[0].shape[0]
    d = g_ref.shape[1]
    shift, scale, _ = _mod_rows(mod_ref, d)
    g1s = g_ref[...] * (1.0 + scale)
    for hf in range(halves):
        for j in range(tn):
            c0 = (hf * tn + j) * r
            hbf_ref[c0:c0 + r, :] = _norm_mod(x_refs[hf][:, j * d:(j + 1) * d], g1s, shift).astype(BF16)
    xb_all = jnp.dot(hbf_ref[...], wxb_ref[...], preferred_element_type=F32).astype(BF16)
    for j in range(tn):
        parts = []
        for hf in range(halves):
            c0 = (hf * tn + j) * r
            pq = jnp.dot(fs1_ref[...], xb_all[c0:c0 + r, :], preferred_element_type=F32)
            p, q = pq[:r], pq[r:]
            tc = tw_refs[hf][:, j:j + 1]
            ts = tw_refs[hf][:, tn + j:tn + j + 1]
            parts.append((p * tc - q * ts, p * ts + q * tc))
        if halves == 1:
            o_ref[0, 0, j] = parts[0][0].astype(BF16)
            o_ref[1, 0, j] = parts[0][1].astype(BF16)
        else:
            for ri in range(2):
                o_ref[ri, 0, j] = (parts[0][ri] + parts[1][ri]).astype(BF16)
                o_ref[ri, 1, j] = (parts[0][ri] - parts[1][ri]).astype(BF16)


def _fft_s1(x, mod, norm_g, w_xb, fs1, tw, *, tn, halves):
    bsz, r, ld = x.shape
    d = ld // TOK
    nc = TOK // halves
    nblk = nc // tn
    const = lambda b, i: (0, 0)
    x_specs = [pl.BlockSpec((None, r, tn * d), functools.partial(lambda b, i, hf: (b, 0, hf * nblk + i), hf=hf))
               for hf in range(halves)]
    tw_specs = [pl.BlockSpec((None, r, 2 * tn), functools.partial(lambda b, i, hf: (hf * nblk + i, 0, 0), hf=hf))
                for hf in range(halves)]
    return pl.pallas_call(
        functools.partial(_fft_s1_kernel, tn=tn, halves=halves),
        out_shape=jax.ShapeDtypeStruct((bsz, 2, halves, nc, r, d), BF16),
        grid=(bsz, nblk),
        in_specs=x_specs + tw_specs + [
            pl.BlockSpec((None, 1, 3 * d), lambda b, i: (b, 0, 0)),
            _resident((1, d), const),
            _resident((d, d), const),
            _resident((2 * r, r), const)],
        out_specs=pl.BlockSpec((None, 2, halves, tn, r, d), lambda b, i: (b, 0, 0, i, 0, 0)),
        scratch_shapes=[pltpu.VMEM((halves * tn * r, d), BF16)],
        compiler_params=_cparams(2),
        name="fft_s1",
    )(*([x] * halves), *([tw] * halves), mod, norm_g, w_xb, fs1)


def _fft_s2_kernel(a_ref, x_ref, ya_ref, mod_ref, g_ref, wgm_ref, m2_ref, cc_ref, cs_ref, wbo_ref, wo_ref,
                   fg_ref, o_ref, *, tk1, final):
    r = x_ref.shape[0]
    d = g_ref.shape[1]
    gw = d // N_FOURIER_GROUPS
    shift, scale, gate = _mod_rows(mod_ref, d)
    g1s = g_ref[...] * (1.0 + scale)
    cols = [slice(q * d, (q + 1) * d) for q in range(tk1)]

    rhs = a_ref[...].reshape(2 * r, tk1 * d)
    u = jnp.dot(m2_ref[...], rhs, preferred_element_type=F32)
    ur = jnp.concatenate([u[:r, cs] for cs in cols], axis=0).astype(BF16)
    uq = jnp.concatenate([u[r:, cs] for cs in cols], axis=0).astype(BF16)
    x = jnp.concatenate([x_ref[:, cs] for cs in cols], axis=0)
    ya = jnp.concatenate([ya_ref[:, cs] for cs in cols], axis=0)

    hn = _norm_mod(x, g1s, shift).astype(BF16)
    gm = jnp.dot(hn, wgm_ref[...], preferred_element_type=F32)
    y = jnp.concatenate(
        [jnp.dot(ur[:, gi * gw:(gi + 1) * gw], cc_ref[...], preferred_element_type=F32)
         + jnp.dot(uq[:, gi * gw:(gi + 1) * gw], cs_ref[...], preferred_element_type=F32)
         for gi in range(N_FOURIER_GROUPS)], axis=1)
    yb = jnp.dot((y * _silu(gm[:, :d])).astype(BF16), wbo_ref[...], preferred_element_type=F32)
    mix = _sigmoid(gm[:, d:2 * d]) * ya + _sigmoid(gm[:, 2 * d:]) * yb
    out = jnp.dot(mix.astype(BF16), wo_ref[...], preferred_element_type=F32)
    xn = x + gate * out
    if final:
        ms = jnp.mean(xn * xn, axis=-1, keepdims=True)
        xn = xn * lax.rsqrt(ms + EPS) * fg_ref[...]
    for q, cs in enumerate(cols):
        if len(o_ref.shape) == 3:
            o_ref[:, q, :] = xn[q * r:(q + 1) * r, :]
        else:
            o_ref[:, cs] = xn[q * r:(q + 1) * r, :]


def _fft_s2(a, x, ya, mod, norm_g, w_gm, m2, cc, cs, w_b_out, w_o, final_g, *, tk1, final):
    bsz, r, ld = x.shape
    d = ld // TOK
    halves = a.shape[2]
    nc = a.shape[3]
    nblk = (TOK // halves) // tk1
    a5 = a.reshape(bsz, 2, halves, nc, (TOK // halves) * d)
    const = lambda b, k2, i: (0, 0)
    tok = pl.BlockSpec((None, r, tk1 * d), lambda b, k2, i: (b, 0, k2 * nblk + i))
    gw = d // N_FOURIER_GROUPS
    if final and tk1 % SUBLANES == 0:
        out_shape = jax.ShapeDtypeStruct((bsz, r, TOK, d), F32)
        out_spec = pl.BlockSpec((None, r, tk1, d), lambda b, k2, i: (b, 0, k2 * nblk + i, 0))
    else:
        out_shape = jax.ShapeDtypeStruct((bsz, r, ld), F32)
        out_spec = tok
    return pl.pallas_call(
        functools.partial(_fft_s2_kernel, tk1=tk1, final=final),
        out_shape=out_shape,
        grid=(bsz, halves, nblk),
        in_specs=[pl.BlockSpec((None, 2, None, nc, tk1 * d), lambda b, k2, i: (b, 0, k2, 0, i)),
                  tok, tok,
                  pl.BlockSpec((None, 1, 3 * d), lambda b, k2, i: (b, 0, 0)),
                  _resident((1, d), const),
                  _resident((d, 3 * d), const),
                  pl.BlockSpec((None, 2 * nc, 2 * nc), lambda b, k2, i: (k2, 0, 0)),
                  _resident((gw, gw), const),
                  _resident((gw, gw), const),
                  _resident((d, d), const),
                  _resident((d, d), const),
                  _resident((1, d), const)],
        out_specs=out_spec,
        compiler_params=_cparams(3),
        name="fft_s2",
    )(a5, x, ya, mod, norm_g, w_gm, m2, cc, cs, w_b_out, w_o, final_g)


def _dft_tables(r, tn, gw):
    halves = TOK // r
    s = r * TOK
    k1 = np.arange(r)
    ang1 = 2.0 * np.pi * np.outer(k1, k1) / r
    fs1 = np.concatenate([np.cos(ang1), np.sin(ang1)], axis=0)
    tok = np.arange(TOK)
    ang_t = 2.0 * np.pi * np.outer(k1, tok) / s
    scale = 1.0 / np.sqrt(float(s) * gw)
    tc = (np.cos(ang_t) * scale).reshape(r, TOK // tn, tn).transpose(1, 0, 2)
    ts = (np.sin(ang_t) * scale).reshape(r, TOK // tn, tn).transpose(1, 0, 2)
    tw = np.concatenate([tc, ts], axis=2)
    n = np.arange(r)
    m2 = []
    for k2 in range(halves):
        ang2 = 2.0 * np.pi * np.outer(halves * n + k2, n) / (halves * r)
        c2, s2 = np.cos(ang2), np.sin(ang2)
        m2.append(np.block([[c2, -s2], [s2, c2]]))
    c = np.arange(gw)
    angc = 2.0 * np.pi * np.outer(c, c) / gw
    return (jnp.asarray(fs1, BF16), jnp.asarray(tw, F32), jnp.asarray(np.stack(m2), BF16),
            jnp.asarray(np.cos(angc), BF16), jnp.asarray(-np.sin(angc), BF16))


def _trunk(x, mods, lw, final_g):
    bsz, s, d = x.shape
    r = s // TOK
    halves = TOK // r
    assert r * TOK == s and r * halves == TOK and halves in (1, 2) and r % SUBLANES == 0
    tn = 8
    tk1 = 512 // r
    gw = d // N_FOURIER_GROUPS
    fs1, tw, m2, cc, cs = _dft_tables(r, tn, gw)
    x = x.reshape(bsz, r, TOK * d)
    depth = len(lw)
    for l in range(depth):
        w = lw[l]
        mod = mods[l]
        xc, hf = _lru_fwd(x, mod, w["norm_g"], w["w_xa"], w["conv_w"], w["conv_b"], w["wg"][0], w["bg"][0],
                          w["lam"][0])
        ya = _lru_bwd(x, xc, hf, mod, w["norm_g"], w["w_ga"], w["wg"][1], w["bg"][1], w["lam"][1],
                      w["w_a_out"])
        a = _fft_s1(x, mod, w["norm_g"], w["w_xb"], fs1, tw, tn=tn, halves=halves)
        x = _fft_s2(a, x, ya, mod, w["norm_g"], w["w_gm"], m2, cc, cs, w["w_b_out"], w["w_o"], final_g,
                    tk1=tk1, final=(l == depth - 1))
    return x.reshape(bsz, s, d)


def _layer_weights(norm_g, w_in, conv_w, conv_b, w_rg, b_rg, lam, w_a_out, w_b_out, w_o):
    depth, d = norm_g.shape
    lw = []
    for l in range(depth):
        wi = w_in[l].astype(BF16)
        lw.append(dict(
            norm_g=norm_g[l].reshape(1, d),
            w_xa=wi[:, 0:d], w_ga=wi[:, d:2 * d], w_xb=wi[:, 2 * d:3 * d], w_gm=wi[:, 3 * d:6 * d],
            conv_w=conv_w[l], conv_b=conv_b[l].reshape(1, d),
            wg=[jnp.concatenate([w_rg[l, dr, 0], w_rg[l, dr, 1]], axis=-1).astype(BF16) for dr in range(2)],
            bg=[b_rg[l, dr] for dr in range(2)],
            lam=[lam[l, dr].reshape(1, d) for dr in range(2)],
            w_a_out=w_a_out[l].astype(BF16), w_b_out=w_b_out[l].astype(BF16), w_o=w_o[l].astype(BF16)))
    return lw


def kernel(x_prompt, x_sample, c_prompt, c_sample, norm_g, w_ada, b_ada, w_in, conv_w, conv_b, w_rg, b_rg, lam,
           w_a_out, w_b_out, w_o, final_g):
    d = x_prompt.shape[-1]
    depth = norm_g.shape[0]
    bp, bs = c_prompt.shape[0], c_sample.shape[0]
    rows = -(-(bp + bs) // SUBLANES) * SUBLANES
    c_all = jnp.concatenate([c_prompt, c_sample, jnp.zeros((rows - bp - bs, d), F32)], axis=0)
    mod = _adaln_mod(c_all, w_ada, b_ada)
    mods_p = [mod[l, 0:bp].reshape(bp, 1, 3 * d) for l in range(depth)]
    mods_s = [mod[l, bp:bp + bs].reshape(bs, 1, 3 * d) for l in range(depth)]
    lw = _layer_weights(norm_g, w_in, conv_w, conv_b, w_rg, b_rg, lam, w_a_out, w_b_out, w_o)
    fg = final_g.reshape(1, d)
    return (_trunk(x_prompt, mods_p, lw, fg), _trunk(x_sample, mods_s, lw, fg))
```

```python
import functools

import numpy as np
import jax
import jax.numpy as jnp
from jax import lax
from jax.experimental import pallas as pl
from jax.experimental.pallas import tpu as pltpu

F32 = jnp.float32
BF16 = jnp.bfloat16

N_LRU_HEADS = 8
N_FOURIER_GROUPS = 4
CONV_WIDTH = 4
CONV_LEFT = 2
LRU_C = 8.0
EPS = 1e-6

SUBLANES = 8
TOK = 128
ROW_CHUNK = 256
VMEM_LIMIT = 58 * 1024 * 1024
TINY = 1e-30
LOG2E = 1.4426950408889634


def _cparams(n_axes):
    return pltpu.CompilerParams(dimension_semantics=("arbitrary",) * n_axes,
                                vmem_limit_bytes=VMEM_LIMIT)


def _resident(shape, index_map):
    return pl.BlockSpec(shape, index_map, pipeline_mode=pl.Buffered(1))


def _sigmoid(z):
    return 0.5 * jnp.tanh(0.5 * z) + 0.5


def _silu(z):
    return z * _sigmoid(z)


def _softplus(z):
    return jnp.maximum(z, 0.0) + jnp.log1p(jnp.exp(-jnp.abs(z)))


def _norm_mod(x, g1s, shift):
    ms = jnp.mean(x * x, axis=-1, keepdims=True)
    return x * lax.rsqrt(ms + EPS) * g1s + shift


def _mod_rows(mod_ref, d):
    return mod_ref[:, 0:d], mod_ref[:, d:2 * d], mod_ref[:, 2 * d:3 * d]


def _norm_to_rows(x_ref, hbf_ref, g1s, shift, n_groups):
    d = g1s.shape[1]
    g = SUBLANES
    for l in range(0, n_groups, 2):
        xx = jnp.concatenate([x_ref[:, l * d:(l + 1) * d], x_ref[:, (l + 1) * d:(l + 2) * d]], axis=0)
        hbf_ref[l * g:(l + 2) * g, :] = _norm_mod(xx, g1s, shift).astype(BF16)


def _mod_kernel(c_ref, w_ref, b_ref, o_ref):
    c = c_ref[...]
    o_ref[...] = jnp.dot(_silu(c), w_ref[...], preferred_element_type=F32,
                         precision=lax.Precision.HIGHEST) + b_ref[...]


def _adaln_mod(c_all, w_ada, b_ada):
    depth, d, d3 = w_ada.shape
    rows = c_all.shape[0]
    return pl.pallas_call(
        _mod_kernel,
        out_shape=jax.ShapeDtypeStruct((depth, rows, d3), F32),
        grid=(depth, d3 // d),
        in_specs=[pl.BlockSpec((rows, d), lambda l, j: (0, 0)),
                  pl.BlockSpec((None, d, d), lambda l, j: (l, 0, j)),
                  pl.BlockSpec((None, 1, d), lambda l, j: (l, 0, j))],
        out_specs=pl.BlockSpec((None, rows, d), lambda l, j: (l, 0, j)),
        compiler_params=_cparams(2),
        name="adaln_mod",
    )(c_all, w_ada, b_ada.reshape(depth, 1, d3))


def _lru_coeffs(xh, gh, hbg_r, hbg_i, k2):
    hb = xh.shape[1]
    t_r = jnp.tanh(gh[:, :hb] + hbg_r)
    t_i = jnp.tanh(gh[:, hb:] + hbg_i)
    a = jnp.exp2((t_r + 1.0) * k2)
    y = 1.0 - a * a
    mult = y * lax.rsqrt(jnp.maximum(y, TINY))
    u = (t_i + 1.0) * xh
    return a, mult * u, u


def _gate_consts(bg_ref, lam_ref):
    return 0.5 * bg_ref[...], (-0.5 * LRU_C * LOG2E) * _softplus(-lam_ref[...])


def _segment_scan(a_ref, b_ref, reverse):
    g = SUBLANES
    n_groups = a_ref.shape[0] // g
    d = a_ref.shape[1]

    def body(s, hp):
        l = (n_groups - 1 - s) if reverse else s
        r0 = pl.multiple_of(l * g, g)
        a = a_ref[pl.ds(r0, g), :]
        h = a * hp[0] + b_ref[pl.ds(r0, g), :]
        p = a * hp[1]
        b_ref[pl.ds(r0, g), :] = h
        a_ref[pl.ds(r0, g), :] = p
        return (h, p)

    return lax.fori_loop(0, n_groups, body, (jnp.zeros((g, d), F32), jnp.ones((g, d), F32)), unroll=4)


def _segment_carries(h_end, p_end, carry, reverse):
    rows = [None] * SUBLANES
    c = carry
    for s in (range(SUBLANES - 1, -1, -1) if reverse else range(SUBLANES)):
        rows[s] = c
        c = h_end[s:s + 1, :] + p_end[s:s + 1, :] * c
    return jnp.concatenate(rows, axis=0), c


def _tile_rows(c, n_rows):
    return jnp.concatenate([c] * (n_rows // c.shape[0]), axis=0)


def _lru_fwd_kernel(x_ref, xn_ref, mod_ref, g_ref, wxa_ref, cw_ref, cb_ref, wg_ref, bg_ref, lam_ref,
                    xc_ref, hf_ref, hbf_ref, ext_ref, a_ref, b_ref, carry_ref, prev_ref, u_ref, *, n_tiles):
    i = pl.program_id(1)
    g = SUBLANES
    d = g_ref.shape[1]
    n_groups = x_ref.shape[1] // d
    n_rows = n_groups * g
    hb = d // N_LRU_HEADS
    shift, scale, _ = _mod_rows(mod_ref, d)
    g1s = g_ref[...] * (1.0 + scale)

    @pl.when(i == 0)
    def _():
        carry_ref[...] = jnp.zeros_like(carry_ref)
        prev_ref[...] = jnp.zeros_like(prev_ref)

    _norm_to_rows(x_ref, hbf_ref, g1s, shift, n_groups)
    xn = jnp.concatenate([xn_ref[...], xn_ref[...]], axis=0)
    hbf_ref[n_rows:n_rows + 2 * g, :] = _norm_mod(xn, g1s, shift).astype(BF16)
    ext_ref[CONV_LEFT * g:, :] = jnp.dot(hbf_ref[...], wxa_ref[...], preferred_element_type=F32)

    row = lax.broadcasted_iota(jnp.int32, (g, d), 0)
    last1 = ext_ref[(n_groups + 1) * g:(n_groups + 2) * g, :]
    last2 = ext_ref[n_groups * g:(n_groups + 1) * g, :]
    first = ext_ref[CONV_LEFT * g:(CONV_LEFT + 1) * g, :]
    nxt = ext_ref[(n_groups + 2) * g:(n_groups + 3) * g, :]
    ext_ref[g:2 * g, :] = jnp.where(row == 0, pltpu.roll(prev_ref[g:2 * g, :], 1, axis=0),
                                    pltpu.roll(last1, 1, axis=0))
    ext_ref[0:g, :] = jnp.where(row == 0, pltpu.roll(prev_ref[0:g, :], 1, axis=0),
                                pltpu.roll(last2, 1, axis=0))
    prev_ref[0:g, :] = last2
    prev_ref[g:2 * g, :] = last1
    nxt_scale = jnp.where(i == n_tiles - 1, 0.0, 1.0)
    ext_ref[(n_groups + 2) * g:(n_groups + 3) * g, :] = jnp.where(
        row == g - 1, pltpu.roll(nxt, g - 1, axis=0) * nxt_scale, pltpu.roll(first, g - 1, axis=0))

    hbg, k2 = _gate_consts(bg_ref, lam_ref)
    hcw = 0.5 * cw_ref[...]
    hcb = 0.5 * cb_ref[...]
    for h in range(N_LRU_HEADS):
        sl = slice(h * hb, (h + 1) * hb)
        xh = ext_ref[0:n_rows, sl] * hcw[0:1, sl] + hcb[:, sl]
        for kk in range(1, CONV_WIDTH):
            xh = xh + ext_ref[kk * g:kk * g + n_rows, sl] * hcw[kk:kk + 1, sl]
        xhb = xh.astype(BF16)
        xc_ref[:, sl] = xhb
        gh = jnp.dot(xhb, wg_ref[h], preferred_element_type=F32)
        a, b, u = _lru_coeffs(xh, gh, hbg[0:1, sl], hbg[1:2, sl], k2[:, sl])
        a_ref[:, sl] = a
        b_ref[:, sl] = b
        u_ref[:, sl] = u[0:g, :]

    @pl.when(i == 0)
    def _():
        b_ref[0:g, :] = jnp.where(row == 0, u_ref[...], b_ref[0:g, :])

    h_end, p_end = _segment_scan(a_ref, b_ref, False)
    c_in, carry_ref[...] = _segment_carries(h_end, p_end, carry_ref[...], False)
    c_rows = _tile_rows(c_in, ROW_CHUNK)
    for r0 in range(0, n_rows, ROW_CHUNK):
        rs = slice(r0, r0 + ROW_CHUNK)
        hf_ref[rs, :] = (b_ref[rs, :] + a_ref[rs, :] * c_rows).astype(BF16)


def _lru_fwd(x, mod, norm_g, w_xa, conv_w, conv_b, wg, bg, lam):
    bsz, r, ld = x.shape
    d = ld // TOK
    g = SUBLANES
    n_tiles = r // g
    n_rows = TOK * g
    hb = d // N_LRU_HEADS
    const = lambda b, i: (0, 0)
    rowform = jax.ShapeDtypeStruct((bsz, n_tiles, n_rows, d), BF16)
    return pl.pallas_call(
        functools.partial(_lru_fwd_kernel, n_tiles=n_tiles),
        out_shape=(rowform, rowform),
        grid=(bsz, n_tiles),
        in_specs=[pl.BlockSpec((None, g, ld), lambda b, i: (b, i, 0)),
                  pl.BlockSpec((None, g, d), lambda b, i: (b, jnp.minimum(i + 1, n_tiles - 1), 0)),
                  pl.BlockSpec((None, 1, 3 * d), lambda b, i: (b, 0, 0)),
                  _resident((1, d), const),
                  _resident((d, d), const),
                  _resident((CONV_WIDTH, d), const),
                  _resident((1, d), const),
                  _resident((N_LRU_HEADS, hb, 2 * hb), lambda b, i: (0, 0, 0)),
                  _resident((2, d), const),
                  _resident((1, d), const)],
        out_specs=(pl.BlockSpec((None, None, n_rows, d), lambda b, i: (b, i, 0, 0)),
                   pl.BlockSpec((None, None, n_rows, d), lambda b, i: (b, i, 0, 0))),
        scratch_shapes=[pltpu.VMEM((n_rows + 2 * g, d), BF16),
                        pltpu.VMEM((n_rows + (CONV_LEFT + 2) * g, d), F32),
                        pltpu.VMEM((n_rows, d), F32),
                        pltpu.VMEM((n_rows, d), F32),
                        pltpu.VMEM((1, d), F32),
                        pltpu.VMEM((CONV_LEFT * g, d), F32),
                        pltpu.VMEM((g, d), F32)],
        compiler_params=_cparams(2),
        name="lru_fwd",
    )(x, x, mod, norm_g, w_xa, conv_w, conv_b, wg, bg, lam)


def _lru_bwd_kernel(x_ref, xc_ref, hf_ref, mod_ref, g_ref, wga_ref, wg_ref, bg_ref, lam_ref, wao_ref,
                    ya_ref, hbf_ref, a_ref, b_ref, carry_ref, u_ref):
    i = pl.program_id(1)
    g = SUBLANES
    d = g_ref.shape[1]
    n_groups = x_ref.shape[1] // d
    n_rows = n_groups * g
    hb = d // N_LRU_HEADS
    shift, scale, _ = _mod_rows(mod_ref, d)
    g1s = g_ref[...] * (1.0 + scale)

    @pl.when(i == 0)
    def _():
        carry_ref[...] = jnp.zeros_like(carry_ref)

    hbg, k2 = _gate_consts(bg_ref, lam_ref)
    for h in range(N_LRU_HEADS):
        sl = slice(h * hb, (h + 1) * hb)
        xhb = xc_ref[:, sl]
        gh = jnp.dot(xhb, wg_ref[h], preferred_element_type=F32)
        a, b, u = _lru_coeffs(xhb.astype(F32), gh, hbg[0:1, sl], hbg[1:2, sl], k2[:, sl])
        a_ref[:, sl] = a
        b_ref[:, sl] = b
        u_ref[:, sl] = u[n_rows - g:, :]

    @pl.when(i == 0)
    def _():
        row = lax.broadcasted_iota(jnp.int32, (g, d), 0)
        b_ref[n_rows - g:, :] = jnp.where(row == g - 1, u_ref[...], b_ref[n_rows - g:, :])

    h_end, p_end = _segment_scan(a_ref, b_ref, True)
    c_in, carry_ref[...] = _segment_carries(h_end, p_end, carry_ref[...], True)
    c_rows = _tile_rows(c_in, ROW_CHUNK)

    _norm_to_rows(x_ref, hbf_ref, g1s, shift, n_groups)
    for r0 in range(0, n_rows, ROW_CHUNK):
        rs = slice(r0, r0 + ROW_CHUNK)
        ga = jnp.dot(hbf_ref[rs, :], wga_ref[...], preferred_element_type=F32)
        hbwd = b_ref[rs, :] + a_ref[rs, :] * c_rows
        ya = (hf_ref[rs, :].astype(F32) + hbwd) * _silu(ga)
        out = jnp.dot(ya.astype(BF16), wao_ref[...], preferred_element_type=F32)
        for gl in range(ROW_CHUNK // g):
            l = r0 // g + gl
            ya_ref[:, l * d:(l + 1) * d] = out[gl * g:(gl + 1) * g, :]


def _lru_bwd(x, xc, hf, mod, norm_g, w_ga, wg, bg, lam, w_a_out):
    bsz, r, ld = x.shape
    d = ld // TOK
    g = SUBLANES
    n_tiles = r // g
    n_rows = TOK * g
    hb = d // N_LRU_HEADS
    const = lambda b, i: (0, 0)
    rev3 = lambda b, i: (b, n_tiles - 1 - i, 0)
    rev4 = lambda b, i: (b, n_tiles - 1 - i, 0, 0)
    return pl.pallas_call(
        _lru_bwd_kernel,
        out_shape=jax.ShapeDtypeStruct((bsz, r, ld), F32),
        grid=(bsz, n_tiles),
        in_specs=[pl.BlockSpec((None, g, ld), rev3),
                  pl.BlockSpec((None, None, n_rows, d), rev4),
                  pl.BlockSpec((None, None, n_rows, d), rev4),
                  pl.BlockSpec((None, 1, 3 * d), lambda b, i: (b, 0, 0)),
                  _resident((1, d), const),
                  _resident((d, d), const),
                  _resident((N_LRU_HEADS, hb, 2 * hb), lambda b, i: (0, 0, 0)),
                  _resident((2, d), const),
                  _resident((1, d), const),
                  _resident((d, d), const)],
        out_specs=pl.BlockSpec((None, g, ld), rev3),
        scratch_shapes=[pltpu.VMEM((n_rows, d), BF16),
                        pltpu.VMEM((n_rows, d), F32),
                        pltpu.VMEM((n_rows, d), F32),
                        pltpu.VMEM((1, d), F32),
                        pltpu.VMEM((g, d), F32)],
        compiler_params=_cparams(2),
        name="lru_bwd",
    )(x, xc, hf, mod, norm_g, w_ga, wg, bg, lam, w_a_out)


def _pack_complex(re, im):
    hi = lax.bitcast_convert_type(re.astype(BF16).astype(F32), jnp.uint32)
    lo = lax.bitcast_convert_type(im.astype(BF16).astype(F32), jnp.uint32)
    return hi | (lo >> 16)


def _unpack_complex(w):
    re = lax.bitcast_convert_type(w & jnp.uint32(0xFFFF0000), F32)
    im = lax.bitcast_convert_type(w << 16, F32)
    return re.astype(BF16), im.astype(BF16)


def _fft_s1_kernel(*refs, tn, halves):
    x_refs = refs[:halves]
    tw_refs = refs[halves:2 * halves]
    mod_ref, g_ref, wxb_ref, fs1_ref, o_ref, hbf_ref = refs[2 * halves:]
    r = x_refs[0].shape[0]
    d = g_ref.shape[1]
    shift, scale, _ = _mod_rows(mod_ref, d)
    g1s = g_ref[...] * (1.0 + scale)
    for hf in range(halves):
        for j in range(tn):
            c0 = (hf * tn + j) * r
            hbf_ref[c0:c0 + r, :] = _norm_mod(x_refs[hf][:, j * d:(j + 1) * d], g1s, shift).astype(BF16)
    xb_all = jnp.dot(hbf_ref[...], wxb_ref[...], preferred_element_type=F32).astype(BF16)
    for j in range(tn):
        parts = []
        for hf in range(halves):
            c0 = (hf * tn + j) * r
            pq = jnp.dot(fs1_ref[...], xb_all[c0:c0 + r, :], preferred_element_type=F32)
            p, q = pq[:r], pq[r:]
            tc = tw_refs[hf][:, j:j + 1]
            ts = tw_refs[hf][:, tn + j:tn + j + 1]
            parts.append((p * tc - q * ts, p * ts + q * tc))
        if halves == 1:
            o_ref[0, :, j, :] = _pack_complex(*parts[0])
        else:
            o_ref[0, :, j, :] = _pack_complex(parts[0][0] + parts[1][0], parts[0][1] + parts[1][1])
            o_ref[1, :, j, :] = _pack_complex(parts[0][0] - parts[1][0], parts[0][1] - parts[1][1])


def _fft_s1(x, mod, norm_g, w_xb, fs1, tw, *, tn, halves):
    bsz, r, ld = x.shape
    d = ld // TOK
    nc = TOK // halves
    nblk = nc // tn
    const = lambda b, i: (0, 0)
    x_specs = [pl.BlockSpec((None, r, tn * d), functools.partial(lambda b, i, hf: (b, 0, hf * nblk + i), hf=hf))
               for hf in range(halves)]
    tw_specs = [pl.BlockSpec((None, r, 2 * tn), functools.partial(lambda b, i, hf: (hf * nblk + i, 0, 0), hf=hf))
                for hf in range(halves)]
    return pl.pallas_call(
        functools.partial(_fft_s1_kernel, tn=tn, halves=halves),
        out_shape=jax.ShapeDtypeStruct((bsz, halves, r, nc, d), jnp.uint32),
        grid=(bsz, nblk),
        in_specs=x_specs + tw_specs + [
            pl.BlockSpec((None, 1, 3 * d), lambda b, i: (b, 0, 0)),
            _resident((1, d), const),
            _resident((d, d), const),
            _resident((2 * r, r), const)],
        out_specs=pl.BlockSpec((None, halves, r, tn, d), lambda b, i: (b, 0, 0, i, 0)),
        scratch_shapes=[pltpu.VMEM((halves * tn * r, d), BF16)],
        compiler_params=_cparams(2),
        name="fft_s1",
    )(*([x] * halves), *([tw] * halves), mod, norm_g, w_xb, fs1)


def _fft_s2_kernel(a_ref, x_ref, ya_ref, mod_ref, g_ref, wgm_ref, m2_ref, cc_ref, cs_ref, wbo_ref, wo_ref,
                   fg_ref, o_ref, *, tk1, final):
    r = x_ref.shape[0]
    d = g_ref.shape[1]
    gw = d // N_FOURIER_GROUPS
    shift, scale, gate = _mod_rows(mod_ref, d)
    g1s = g_ref[...] * (1.0 + scale)
    cols = [slice(q * d, (q + 1) * d) for q in range(tk1)]

    us = [jnp.dot(m2_ref[...], jnp.concatenate(_unpack_complex(a_ref[q]), axis=0),
                  preferred_element_type=F32) for q in range(tk1)]
    ur = jnp.concatenate([u[:r] for u in us], axis=0).astype(BF16)
    uq = jnp.concatenate([u[r:] for u in us], axis=0).astype(BF16)
    x = jnp.concatenate([x_ref[:, cs] for cs in cols], axis=0)
    ya = jnp.concatenate([ya_ref[:, cs] for cs in cols], axis=0)

    hn = _norm_mod(x, g1s, shift).astype(BF16)
    gm = jnp.dot(hn, wgm_ref[...], preferred_element_type=F32)
    y = jnp.concatenate(
        [jnp.dot(ur[:, gi * gw:(gi + 1) * gw], cc_ref[...], preferred_element_type=F32)
         + jnp.dot(uq[:, gi * gw:(gi + 1) * gw], cs_ref[...], preferred_element_type=F32)
         for gi in range(N_FOURIER_GROUPS)], axis=1)
    yb = jnp.dot((y * _silu(gm[:, :d])).astype(BF16), wbo_ref[...], preferred_element_type=F32)
    mix = _sigmoid(gm[:, d:2 * d]) * ya + _sigmoid(gm[:, 2 * d:]) * yb
    out = jnp.dot(mix.astype(BF16), wo_ref[...], preferred_element_type=F32)
    xn = x + gate * out
    if final:
        ms = jnp.mean(xn * xn, axis=-1, keepdims=True)
        xn = xn * lax.rsqrt(ms + EPS) * fg_ref[...]
    for q, cs in enumerate(cols):
        if len(o_ref.shape) == 3:
            o_ref[:, q, :] = xn[q * r:(q + 1) * r, :]
        else:
            o_ref[:, cs] = xn[q * r:(q + 1) * r, :]


def _fft_s2(a, x, ya, mod, norm_g, w_gm, m2, cc, cs, w_b_out, w_o, final_g, *, tk1, final):
    bsz, r, ld = x.shape
    d = ld // TOK
    halves = a.shape[1]
    nc = a.shape[3]
    nblk = (TOK // halves) // tk1
    const = lambda b, k2, i: (0, 0)
    tok = pl.BlockSpec((None, r, tk1 * d), lambda b, k2, i: (b, 0, k2 * nblk + i))
    gw = d // N_FOURIER_GROUPS
    if final and tk1 % SUBLANES == 0:
        out_shape = jax.ShapeDtypeStruct((bsz, r, TOK, d), F32)
        out_spec = pl.BlockSpec((None, r, tk1, d), lambda b, k2, i: (b, 0, k2 * nblk + i, 0))
    else:
        out_shape = jax.ShapeDtypeStruct((bsz, r, ld), F32)
        out_spec = tok
    return pl.pallas_call(
        functools.partial(_fft_s2_kernel, tk1=tk1, final=final),
        out_shape=out_shape,
        grid=(bsz, halves, nblk),
        in_specs=[pl.BlockSpec((None, None, tk1, nc, d), lambda b, k2, i: (b, k2, i, 0, 0)),
                  tok, tok,
                  pl.BlockSpec((None, 1, 3 * d), lambda b, k2, i: (b, 0, 0)),
                  _resident((1, d), const),
                  _resident((d, 3 * d), const),
                  pl.BlockSpec((None, 2 * nc, 2 * nc), lambda b, k2, i: (k2, 0, 0)),
                  _resident((gw, gw), const),
                  _resident((gw, gw), const),
                  _resident((d, d), const),
                  _resident((d, d), const),
                  _resident((1, d), const)],
        out_specs=out_spec,
        compiler_params=_cparams(3),
        name="fft_s2",
    )(a, x, ya, mod, norm_g, w_gm, m2, cc, cs, w_b_out, w_o, final_g)


def _dft_tables(r, tn, gw):
    halves = TOK // r
    s = r * TOK
    k1 = np.arange(r)
    ang1 = 2.0 * np.pi * np.outer(k1, k1) / r
    fs1 = np.concatenate([np.cos(ang1), np.sin(ang1)], axis=0)
    tok = np.arange(TOK)
    ang_t = 2.0 * np.pi * np.outer(k1, tok) / s
    scale = 1.0 / np.sqrt(float(s) * gw)
    tc = (np.cos(ang_t) * scale).reshape(r, TOK // tn, tn).transpose(1, 0, 2)
    ts = (np.sin(ang_t) * scale).reshape(r, TOK // tn, tn).transpose(1, 0, 2)
    tw = np.concatenate([tc, ts], axis=2)
    n = np.arange(r)
    m2 = []
    for k2 in range(halves):
        ang2 = 2.0 * np.pi * np.outer(halves * n + k2, n) / (halves * r)
        c2, s2 = np.cos(ang2), np.sin(ang2)
        m2.append(np.block([[c2, -s2], [s2, c2]]))
    c = np.arange(gw)
    angc = 2.0 * np.pi * np.outer(c, c) / gw
    return (jnp.asarray(fs1, BF16), jnp.asarray(tw, F32), jnp.asarray(np.stack(m2), BF16),
            jnp.asarray(np.cos(angc), BF16), jnp.asarray(-np.sin(angc), BF16))


def _trunk(x, mods, lw, final_g):
    bsz, s, d = x.shape
    r = s // TOK
    halves = TOK // r
    assert r * TOK == s and r * halves == TOK and halves in (1, 2) and r % SUBLANES == 0
    tn = 8
    tk1 = 512 // r
    gw = d // N_FOURIER_GROUPS
    fs1, tw, m2, cc, cs = _dft_tables(r, tn, gw)
    x = x.reshape(bsz, r, TOK * d)
    depth = len(lw)
    for l in range(depth):
        w = lw[l]
        mod = mods[l]
        xc, hf = _lru_fwd(x, mod, w["norm_g"], w["w_xa"], w["conv_w"], w["conv_b"], w["wg"][0], w["bg"][0],
                          w["lam"][0])
        ya = _lru_bwd(x, xc, hf, mod, w["norm_g"], w["w_ga"], w["wg"][1], w["bg"][1], w["lam"][1],
                      w["w_a_out"])
        a = _fft_s1(x, mod, w["norm_g"], w["w_xb"], fs1, tw, tn=tn, halves=halves)
        x = _fft_s2(a, x, ya, mod, w["norm_g"], w["w_gm"], m2, cc, cs, w["w_b_out"], w["w_o"], final_g,
                    tk1=tk1, final=(l == depth - 1))
    return x.reshape(bsz, s, d)


def _layer_weights(norm_g, w_in, conv_w, conv_b, w_rg, b_rg, lam, w_a_out, w_b_out, w_o):
    depth, d = norm_g.shape
    lw = []
    for l in range(depth):
        wi = w_in[l].astype(BF16)
        lw.append(dict(
            norm_g=norm_g[l].reshape(1, d),
            w_xa=wi[:, 0:d], w_ga=wi[:, d:2 * d], w_xb=wi[:, 2 * d:3 * d], w_gm=wi[:, 3 * d:6 * d],
            conv_w=conv_w[l], conv_b=conv_b[l].reshape(1, d),
            wg=[jnp.concatenate([w_rg[l, dr, 0], w_rg[l, dr, 1]], axis=-1).astype(BF16) for dr in range(2)],
            bg=[b_rg[l, dr] for dr in range(2)],
            lam=[lam[l, dr].reshape(1, d) for dr in range(2)],
            w_a_out=w_a_out[l].astype(BF16), w_b_out=w_b_out[l].astype(BF16), w_o=w_o[l].astype(BF16)))
    return lw


def kernel(x_prompt, x_sample, c_prompt, c_sample, norm_g, w_ada, b_ada, w_in, conv_w, conv_b, w_rg, b_rg, lam,
           w_a_out, w_b_out, w_o, final_g):
    d = x_prompt.shape[-1]
    depth = norm_g.shape[0]
    bp, bs = c_prompt.shape[0], c_sample.shape[0]
    rows = -(-(bp + bs) // SUBLANES) * SUBLANES
    c_all = jnp.concatenate([c_prompt, c_sample, jnp.zeros((rows - bp - bs, d), F32)], axis=0)
    mod = _adaln_mod(c_all, w_ada, b_ada)
    mods_p = [mod[l, 0:bp].reshape(bp, 1, 3 * d) for l in range(depth)]
    mods_s = [mod[l, bp:bp + bs].reshape(bs, 1, 3 * d) for l in range(depth)]
    lw = _layer_weights(norm_g, w_in, conv_w, conv_b, w_rg, b_rg, lam, w_a_out, w_b_out, w_o)
    fg = final_g.reshape(1, d)
    return (_trunk(x_prompt, mods_p, lw, fg), _trunk(x_sample, mods_s, lw, fg))
```

```python
import functools

import numpy as np
import jax
import jax.numpy as jnp
from jax import lax
from jax.experimental import pallas as pl
from jax.experimental.pallas import tpu as pltpu

F32 = jnp.float32
BF16 = jnp.bfloat16

N_LRU_HEADS = 8
N_FOURIER_GROUPS = 4
CONV_WIDTH = 4
CONV_LEFT = 2
LRU_C = 8.0
EPS = 1e-6

SUBLANES = 8
TOK = 128
ROW_CHUNK = 256
VMEM_LIMIT = 58 * 1024 * 1024
TINY = 1e-30
LOG2E = 1.4426950408889634


def _cparams(n_axes):
    return pltpu.CompilerParams(dimension_semantics=("arbitrary",) * n_axes,
                                vmem_limit_bytes=VMEM_LIMIT)


def _resident(shape, index_map):
    return pl.BlockSpec(shape, index_map, pipeline_mode=pl.Buffered(1))


def _sigmoid(z):
    return 0.5 * jnp.tanh(0.5 * z) + 0.5


def _silu(z):
    zh = 0.5 * z
    return zh * (jnp.tanh(zh) + 1.0)


def _softplus(z):
    return jnp.maximum(z, 0.0) + jnp.log1p(jnp.exp(-jnp.abs(z)))


def _norm_mod(x, g1s, shift):
    ms = jnp.mean(x * x, axis=-1, keepdims=True)
    return x * lax.rsqrt(ms + EPS) * g1s + shift


def _mod_rows(mod_ref, d):
    return mod_ref[:, 0:d], mod_ref[:, d:2 * d], mod_ref[:, 2 * d:3 * d]


def _norm_to_rows(x_ref, hbf_ref, g1s, shift, n_groups, xs_ref=None):
    d = g1s.shape[1]
    g = SUBLANES

    def group(l):
        if xs_ref is None:
            return x_ref[:, l * d:(l + 1) * d]
        xg = x_ref[:, l, :]
        xs_ref[:, l * d:(l + 1) * d] = xg
        return xg

    for l in range(0, n_groups, 2):
        xx = jnp.concatenate([group(l), group(l + 1)], axis=0)
        hbf_ref[l * g:(l + 2) * g, :] = _norm_mod(xx, g1s, shift).astype(BF16)


def _mod_kernel(c_ref, w_ref, b_ref, o_ref):
    c = c_ref[...]
    o_ref[...] = jnp.dot(_silu(c), w_ref[...], preferred_element_type=F32,
                         precision=lax.Precision.HIGHEST) + b_ref[...]


def _adaln_mod(c_all, w_ada, b_ada):
    depth, d, d3 = w_ada.shape
    rows = c_all.shape[0]
    return pl.pallas_call(
        _mod_kernel,
        out_shape=jax.ShapeDtypeStruct((depth, rows, d3), F32),
        grid=(depth, d3 // d),
        in_specs=[pl.BlockSpec((rows, d), lambda l, j: (0, 0)),
                  pl.BlockSpec((None, d, d), lambda l, j: (l, 0, j)),
                  pl.BlockSpec((None, 1, d), lambda l, j: (l, 0, j))],
        out_specs=pl.BlockSpec((None, rows, d), lambda l, j: (l, 0, j)),
        compiler_params=_cparams(2),
        name="adaln_mod",
    )(c_all, w_ada, b_ada.reshape(depth, 1, d3))


def _lru_coeffs(xh, gh, hbg_r, hbg_i, k2):
    hb = xh.shape[1]
    t_r = jnp.tanh(gh[:, :hb] + hbg_r)
    t_i = jnp.tanh(gh[:, hb:] + hbg_i)
    a = jnp.exp2((t_r + 1.0) * k2)
    y = 1.0 - a * a
    mult = y * lax.rsqrt(jnp.maximum(y, TINY))
    u = (t_i + 1.0) * xh
    return a, mult * u, u


def _gate_consts(bg_ref, lam_ref):
    return 0.5 * bg_ref[...], (-0.5 * LRU_C * LOG2E) * _softplus(-lam_ref[...])


def _segment_ends(a_ref, b_ref, reverse):
    g = SUBLANES
    n_groups = a_ref.shape[0] // g
    d = a_ref.shape[1]

    def body(s, hp):
        l = (n_groups - 1 - s) if reverse else s
        r0 = pl.multiple_of(l * g, g)
        a = a_ref[pl.ds(r0, g), :]
        return (a * hp[0] + b_ref[pl.ds(r0, g), :], a * hp[1])

    return lax.fori_loop(0, n_groups, body, (jnp.zeros((g, d), F32), jnp.ones((g, d), F32)), unroll=8)


def _segment_apply(a_ref, b_ref, c_in, out_ref, reverse):
    g = SUBLANES
    n_pairs = a_ref.shape[0] // (2 * g)

    def body(s, h):
        p = (n_pairs - 1 - s) if reverse else s
        r0 = pl.multiple_of(p * 2 * g, 2 * g)
        a = a_ref[pl.ds(r0, 2 * g), :]
        b = b_ref[pl.ds(r0, 2 * g), :]
        if reverse:
            h1 = a[g:] * h + b[g:]
            h0 = a[:g] * h1 + b[:g]
            h = h0
        else:
            h0 = a[:g] * h + b[:g]
            h1 = a[g:] * h0 + b[g:]
            h = h1
        out_ref[pl.ds(r0, 2 * g), :] = jnp.concatenate([h0, h1], axis=0).astype(out_ref.dtype)
        return h

    lax.fori_loop(0, n_pairs, body, c_in, unroll=4)


def _segment_carries(h_end, p_end, carry, reverse):
    rows = [None] * SUBLANES
    c = carry
    for s in (range(SUBLANES - 1, -1, -1) if reverse else range(SUBLANES)):
        rows[s] = c
        c = h_end[s:s + 1, :] + p_end[s:s + 1, :] * c
    return jnp.concatenate(rows, axis=0), c


def _lru_fwd_kernel(*refs, n_tiles, natural_in):
    (x_ref, xn_ref, mod_ref, g_ref, wxa_ref, cw_ref, cb_ref, wg_ref, bg_ref, lam_ref,
     xc_ref, hf_ref, hbf_ref) = refs[:13]
    xs_ref = refs[13] if natural_in else None
    ext_ref, a_ref, b_ref, carry_ref, prev_ref, u_ref = refs[-6:]
    i = pl.program_id(1)
    g = SUBLANES
    d = g_ref.shape[1]
    n_rows = a_ref.shape[0]
    n_groups = n_rows // g
    hb = d // N_LRU_HEADS
    shift, scale, _ = _mod_rows(mod_ref, d)
    g1s = g_ref[...] * (1.0 + scale)

    @pl.when(i == 0)
    def _():
        carry_ref[...] = jnp.zeros_like(carry_ref)
        prev_ref[...] = jnp.zeros_like(prev_ref)

    _norm_to_rows(x_ref, hbf_ref, g1s, shift, n_groups, xs_ref)
    xn = xn_ref[:, 0, :] if natural_in else xn_ref[...]
    hbf_ref[n_rows:n_rows + 2 * g, :] = _norm_mod(jnp.concatenate([xn, xn], axis=0), g1s, shift).astype(BF16)
    ext_ref[CONV_LEFT * g:, :] = jnp.dot(hbf_ref[...], wxa_ref[...], preferred_element_type=F32)

    row = lax.broadcasted_iota(jnp.int32, (g, d), 0)
    last1 = ext_ref[(n_groups + 1) * g:(n_groups + 2) * g, :]
    last2 = ext_ref[n_groups * g:(n_groups + 1) * g, :]
    first = ext_ref[CONV_LEFT * g:(CONV_LEFT + 1) * g, :]
    nxt = ext_ref[(n_groups + 2) * g:(n_groups + 3) * g, :]
    ext_ref[g:2 * g, :] = jnp.where(row == 0, pltpu.roll(prev_ref[g:2 * g, :], 1, axis=0),
                                    pltpu.roll(last1, 1, axis=0))
    ext_ref[0:g, :] = jnp.where(row == 0, pltpu.roll(prev_ref[0:g, :], 1, axis=0),
                                pltpu.roll(last2, 1, axis=0))
    prev_ref[0:g, :] = last2
    prev_ref[g:2 * g, :] = last1
    nxt_scale = jnp.where(i == n_tiles - 1, 0.0, 1.0)
    ext_ref[(n_groups + 2) * g:(n_groups + 3) * g, :] = jnp.where(
        row == g - 1, pltpu.roll(nxt, g - 1, axis=0) * nxt_scale, pltpu.roll(first, g - 1, axis=0))

    hbg, k2 = _gate_consts(bg_ref, lam_ref)
    hcw = 0.5 * cw_ref[...]
    hcb = 0.5 * cb_ref[...]
    for h in range(N_LRU_HEADS):
        sl = slice(h * hb, (h + 1) * hb)
        xh = ext_ref[0:n_rows, sl] * hcw[0:1, sl] + hcb[:, sl]
        for kk in range(1, CONV_WIDTH):
            xh = xh + ext_ref[kk * g:kk * g + n_rows, sl] * hcw[kk:kk + 1, sl]
        xhb = xh.astype(BF16)
        xc_ref[:, sl] = xhb
        gh = jnp.dot(xhb, wg_ref[h], preferred_element_type=F32)
        a, b, u = _lru_coeffs(xh, gh, hbg[0:1, sl], hbg[1:2, sl], k2[:, sl])
        a_ref[:, sl] = a
        b_ref[:, sl] = b
        u_ref[:, sl] = u[0:g, :]

    @pl.when(i == 0)
    def _():
        b_ref[0:g, :] = jnp.where(row == 0, u_ref[...], b_ref[0:g, :])

    h_end, p_end = _segment_ends(a_ref, b_ref, False)
    c_in, carry_ref[...] = _segment_carries(h_end, p_end, carry_ref[...], False)
    _segment_apply(a_ref, b_ref, c_in, hf_ref, False)


def _lru_fwd(x, mod, norm_g, w_xa, conv_w, conv_b, wg, bg, lam, *, natural_in):
    g = SUBLANES
    if natural_in:
        bsz, s, d = x.shape
        r = s // TOK
        x = x.reshape(bsz, r, TOK, d)
        x_spec = pl.BlockSpec((None, g, TOK, d), lambda b, i: (b, i, 0, 0))
        xn_spec = pl.BlockSpec((None, g, g, d), lambda b, i: (b, jnp.minimum(i + 1, n_tiles - 1), 0, 0))
    else:
        bsz, r, ld = x.shape
        d = ld // TOK
        x_spec = pl.BlockSpec((None, g, TOK * d), lambda b, i: (b, i, 0))
        xn_spec = pl.BlockSpec((None, g, d), lambda b, i: (b, jnp.minimum(i + 1, n_tiles - 1), 0))
    n_tiles = r // g
    n_rows = TOK * g
    hb = d // N_LRU_HEADS
    const = lambda b, i: (0, 0)
    rowform = jax.ShapeDtypeStruct((bsz, n_tiles, n_rows, d), BF16)
    rowspec = pl.BlockSpec((None, None, n_rows, d), lambda b, i: (b, i, 0, 0))
    out_shape = [rowform, rowform, jax.ShapeDtypeStruct((bsz, n_tiles, n_rows + 2 * g, d), BF16)]
    out_specs = [rowspec, rowspec, pl.BlockSpec((None, None, n_rows + 2 * g, d), lambda b, i: (b, i, 0, 0))]
    if natural_in:
        out_shape.append(jax.ShapeDtypeStruct((bsz, r, TOK * d), F32))
        out_specs.append(pl.BlockSpec((None, g, TOK * d), lambda b, i: (b, i, 0)))
    return pl.pallas_call(
        functools.partial(_lru_fwd_kernel, n_tiles=n_tiles, natural_in=natural_in),
        out_shape=tuple(out_shape),
        grid=(bsz, n_tiles),
        in_specs=[x_spec, xn_spec,
                  pl.BlockSpec((None, 1, 3 * d), lambda b, i: (b, 0, 0)),
                  _resident((1, d), const),
                  _resident((d, d), const),
                  _resident((CONV_WIDTH, d), const),
                  _resident((1, d), const),
                  _resident((N_LRU_HEADS, hb, 2 * hb), lambda b, i: (0, 0, 0)),
                  _resident((2, d), const),
                  _resident((1, d), const)],
        out_specs=tuple(out_specs),
        scratch_shapes=[pltpu.VMEM((n_rows + (CONV_LEFT + 2) * g, d), F32),
                        pltpu.VMEM((n_rows, d), F32),
                        pltpu.VMEM((n_rows, d), F32),
                        pltpu.VMEM((1, d), F32),
                        pltpu.VMEM((CONV_LEFT * g, d), F32),
                        pltpu.VMEM((g, d), F32)],
        compiler_params=_cparams(2),
        name="lru_fwd",
    )(x, x, mod, norm_g, w_xa, conv_w, conv_b, wg, bg, lam)


def _lru_bwd_kernel(xc_ref, hf_ref, hbf_ref, wga_ref, wg_ref, bg_ref, lam_ref, wao_ref,
                    ya_ref, a_ref, b_ref, carry_ref, u_ref):
    i = pl.program_id(1)
    g = SUBLANES
    n_rows, d = a_ref.shape
    hb = d // N_LRU_HEADS

    @pl.when(i == 0)
    def _():
        carry_ref[...] = jnp.zeros_like(carry_ref)

    hbg, k2 = _gate_consts(bg_ref, lam_ref)
    for h in range(N_LRU_HEADS):
        sl = slice(h * hb, (h + 1) * hb)
        xhb = xc_ref[:, sl]
        gh = jnp.dot(xhb, wg_ref[h], preferred_element_type=F32)
        a, b, u = _lru_coeffs(xhb.astype(F32), gh, hbg[0:1, sl], hbg[1:2, sl], k2[:, sl])
        a_ref[:, sl] = a
        b_ref[:, sl] = b
        u_ref[:, sl] = u[n_rows - g:, :]

    @pl.when(i == 0)
    def _():
        row = lax.broadcasted_iota(jnp.int32, (g, d), 0)
        b_ref[n_rows - g:, :] = jnp.where(row == g - 1, u_ref[...], b_ref[n_rows - g:, :])

    h_end, p_end = _segment_ends(a_ref, b_ref, True)
    c_in, carry_ref[...] = _segment_carries(h_end, p_end, carry_ref[...], True)
    _segment_apply(a_ref, b_ref, c_in, b_ref, True)

    for r0 in range(0, n_rows, ROW_CHUNK):
        rs = slice(r0, r0 + ROW_CHUNK)
        ga = jnp.dot(hbf_ref[rs, :], wga_ref[...], preferred_element_type=F32)
        ya = (hf_ref[rs, :].astype(F32) + b_ref[rs, :]) * _silu(ga)
        out = jnp.dot(ya.astype(BF16), wao_ref[...], preferred_element_type=F32)
        for gl in range(ROW_CHUNK // g):
            l = r0 // g + gl
            ya_ref[:, l * d:(l + 1) * d] = out[gl * g:(gl + 1) * g, :]


def _lru_bwd(xc, hf, hbf, w_ga, wg, bg, lam, w_a_out):
    bsz, n_tiles, n_rows, d = xc.shape
    g = SUBLANES
    hb = d // N_LRU_HEADS
    const = lambda b, i: (0, 0)
    rev3 = lambda b, i: (b, n_tiles - 1 - i, 0)
    rev4 = lambda b, i: (b, n_tiles - 1 - i, 0, 0)
    return pl.pallas_call(
        _lru_bwd_kernel,
        out_shape=jax.ShapeDtypeStruct((bsz, n_tiles * g, TOK * d), F32),
        grid=(bsz, n_tiles),
        in_specs=[pl.BlockSpec((None, None, n_rows, d), rev4),
                  pl.BlockSpec((None, None, n_rows, d), rev4),
                  pl.BlockSpec((None, None, n_rows + 2 * g, d), rev4),
                  _resident((d, d), const),
                  _resident((N_LRU_HEADS, hb, 2 * hb), lambda b, i: (0, 0, 0)),
                  _resident((2, d), const),
                  _resident((1, d), const),
                  _resident((d, d), const)],
        out_specs=pl.BlockSpec((None, g, TOK * d), rev3),
        scratch_shapes=[pltpu.VMEM((n_rows, d), F32),
                        pltpu.VMEM((n_rows, d), F32),
                        pltpu.VMEM((1, d), F32),
                        pltpu.VMEM((g, d), F32)],
        compiler_params=_cparams(2),
        name="lru_bwd",
    )(xc, hf, hbf, w_ga, wg, bg, lam, w_a_out)


def _pack_complex(re, im):
    hi = lax.bitcast_convert_type(re.astype(BF16).astype(F32), jnp.uint32)
    lo = lax.bitcast_convert_type(im.astype(BF16).astype(F32), jnp.uint32)
    return hi | (lo >> 16)


def _unpack_complex(w):
    re = lax.bitcast_convert_type(w & jnp.uint32(0xFFFF0000), F32)
    im = lax.bitcast_convert_type(w << 16, F32)
    return re.astype(BF16), im.astype(BF16)


def _fft_s1_kernel(*refs, tn, halves):
    x_refs = refs[:halves]
    tw_refs = refs[halves:2 * halves]
    mod_ref, g_ref, wxb_ref, fs1_ref, o_ref, hbf_ref = refs[2 * halves:]
    r = x_refs[0].shape[0]
    d = g_ref.shape[1]
    shift, scale, _ = _mod_rows(mod_ref, d)
    g1s = g_ref[...] * (1.0 + scale)
    for hf in range(halves):
        for j in range(tn):
            c0 = (hf * tn + j) * r
            hbf_ref[c0:c0 + r, :] = _norm_mod(x_refs[hf][:, j * d:(j + 1) * d], g1s, shift).astype(BF16)
    xb_all = jnp.dot(hbf_ref[...], wxb_ref[...], preferred_element_type=F32).astype(BF16)
    for j in range(tn):
        parts = []
        for hf in range(halves):
            c0 = (hf * tn + j) * r
            pq = jnp.dot(fs1_ref[...], xb_all[c0:c0 + r, :], preferred_element_type=F32)
            p, q = pq[:r], pq[r:]
            tc = tw_refs[hf][:, j:j + 1]
            ts = tw_refs[hf][:, tn + j:tn + j + 1]
            parts.append((p * tc - q * ts, p * ts + q * tc))
        if halves == 1:
            o_ref[0, :, j, :] = _pack_complex(*parts[0])
        else:
            o_ref[0, :, j, :] = _pack_complex(parts[0][0] + parts[1][0], parts[0][1] + parts[1][1])
            o_ref[1, :, j, :] = _pack_complex(parts[0][0] - parts[1][0], parts[0][1] - parts[1][1])


def _fft_s1(x, mod, norm_g, w_xb, fs1, tw, *, tn, halves):
    bsz, r, ld = x.shape
    d = ld // TOK
    nc = TOK // halves
    nblk = nc // tn
    const = lambda b, i: (0, 0)
    x_specs = [pl.BlockSpec((None, r, tn * d), functools.partial(lambda b, i, hf: (b, 0, hf * nblk + i), hf=hf))
               for hf in range(halves)]
    tw_specs = [pl.BlockSpec((None, r, 2 * tn), functools.partial(lambda b, i, hf: (hf * nblk + i, 0, 0), hf=hf))
                for hf in range(halves)]
    return pl.pallas_call(
        functools.partial(_fft_s1_kernel, tn=tn, halves=halves),
        out_shape=jax.ShapeDtypeStruct((bsz, halves, r, nc, d), jnp.uint32),
        grid=(bsz, nblk),
        in_specs=x_specs + tw_specs + [
            pl.BlockSpec((None, 1, 3 * d), lambda b, i: (b, 0, 0)),
            _resident((1, d), const),
            _resident((d, d), const),
            _resident((2 * r, r), const)],
        out_specs=pl.BlockSpec((None, halves, r, tn, d), lambda b, i: (b, 0, 0, i, 0)),
        scratch_shapes=[pltpu.VMEM((halves * tn * r, d), BF16)],
        compiler_params=_cparams(2),
        name="fft_s1",
    )(*([x] * halves), *([tw] * halves), mod, norm_g, w_xb, fs1)


def _fft_s2_kernel(a_ref, x_ref, ya_ref, mod_ref, g_ref, wgm_ref, m2_ref, cc_ref, cs_ref, wbo_ref, wo_ref,
                   fg_ref, o_ref, *, tk1, final):
    r = x_ref.shape[0]
    d = g_ref.shape[1]
    gw = d // N_FOURIER_GROUPS
    shift, scale, gate = _mod_rows(mod_ref, d)
    g1s = g_ref[...] * (1.0 + scale)
    cols = [slice(q * d, (q + 1) * d) for q in range(tk1)]

    us = [jnp.dot(m2_ref[...], jnp.concatenate(_unpack_complex(a_ref[q]), axis=0),
                  preferred_element_type=F32) for q in range(tk1)]
    ur = jnp.concatenate([u[:r] for u in us], axis=0).astype(BF16)
    uq = jnp.concatenate([u[r:] for u in us], axis=0).astype(BF16)
    x = jnp.concatenate([x_ref[:, cs] for cs in cols], axis=0)
    ya = jnp.concatenate([ya_ref[:, cs] for cs in cols], axis=0)

    hn = _norm_mod(x, g1s, shift).astype(BF16)
    gm = jnp.dot(hn, wgm_ref[...], preferred_element_type=F32)
    y = jnp.concatenate(
        [jnp.dot(ur[:, gi * gw:(gi + 1) * gw], cc_ref[...], preferred_element_type=F32)
         + jnp.dot(uq[:, gi * gw:(gi + 1) * gw], cs_ref[...], preferred_element_type=F32)
         for gi in range(N_FOURIER_GROUPS)], axis=1)
    yb = jnp.dot((y * _silu(gm[:, :d])).astype(BF16), wbo_ref[...], preferred_element_type=F32)
    mix = _sigmoid(gm[:, d:2 * d]) * ya + _sigmoid(gm[:, 2 * d:]) * yb
    out = jnp.dot(mix.astype(BF16), wo_ref[...], preferred_element_type=F32)
    xn = x + gate * out
    if final:
        ms = jnp.mean(xn * xn, axis=-1, keepdims=True)
        xn = xn * lax.rsqrt(ms + EPS) * fg_ref[...]
    for q, cs in enumerate(cols):
        if len(o_ref.shape) == 3:
            o_ref[:, q, :] = xn[q * r:(q + 1) * r, :]
        else:
            o_ref[:, cs] = xn[q * r:(q + 1) * r, :]


def _fft_s2(a, x, ya, mod, norm_g, w_gm, m2, cc, cs, w_b_out, w_o, final_g, *, tk1, final):
    bsz, r, ld = x.shape
    d = ld // TOK
    halves = a.shape[1]
    nc = a.shape[3]
    nblk = (TOK // halves) // tk1
    const = lambda b, k2, i: (0, 0)
    tok = pl.BlockSpec((None, r, tk1 * d), lambda b, k2, i: (b, 0, k2 * nblk + i))
    gw = d // N_FOURIER_GROUPS
    if final and tk1 % SUBLANES == 0:
        out_shape = jax.ShapeDtypeStruct((bsz, r, TOK, d), F32)
        out_spec = pl.BlockSpec((None, r, tk1, d), lambda b, k2, i: (b, 0, k2 * nblk + i, 0))
    else:
        out_shape = jax.ShapeDtypeStruct((bsz, r, ld), F32)
        out_spec = tok
    return pl.pallas_call(
        functools.partial(_fft_s2_kernel, tk1=tk1, final=final),
        out_shape=out_shape,
        grid=(bsz, halves, nblk),
        in_specs=[pl.BlockSpec((None, None, tk1, nc, d), lambda b, k2, i: (b, k2, i, 0, 0)),
                  tok, tok,
                  pl.BlockSpec((None, 1, 3 * d), lambda b, k2, i: (b, 0, 0)),
                  _resident((1, d), const),
                  _resident((d, 3 * d), const),
                  pl.BlockSpec((None, 2 * nc, 2 * nc), lambda b, k2, i: (k2, 0, 0)),
                  _resident((gw, gw), const),
                  _resident((gw, gw), const),
                  _resident((d, d), const),
                  _resident((d, d), const),
                  _resident((1, d), const)],
        out_specs=out_spec,
        compiler_params=_cparams(3),
        name="fft_s2",
    )(a, x, ya, mod, norm_g, w_gm, m2, cc, cs, w_b_out, w_o, final_g)


def _dft_tables(r, tn, gw):
    halves = TOK // r
    s = r * TOK
    k1 = np.arange(r)
    ang1 = 2.0 * np.pi * np.outer(k1, k1) / r
    fs1 = np.concatenate([np.cos(ang1), np.sin(ang1)], axis=0)
    tok = np.arange(TOK)
    ang_t = 2.0 * np.pi * np.outer(k1, tok) / s
    scale = 1.0 / np.sqrt(float(s) * gw)
    tc = (np.cos(ang_t) * scale).reshape(r, TOK // tn, tn).transpose(1, 0, 2)
    ts = (np.sin(ang_t) * scale).reshape(r, TOK // tn, tn).transpose(1, 0, 2)
    tw = np.concatenate([tc, ts], axis=2)
    n = np.arange(r)
    m2 = []
    for k2 in range(halves):
        ang2 = 2.0 * np.pi * np.outer(halves * n + k2, n) / (halves * r)
        c2, s2 = np.cos(ang2), np.sin(ang2)
        m2.append(np.block([[c2, -s2], [s2, c2]]))
    c = np.arange(gw)
    angc = 2.0 * np.pi * np.outer(c, c) / gw
    return (jnp.asarray(fs1, BF16), jnp.asarray(tw, F32), jnp.asarray(np.stack(m2), BF16),
            jnp.asarray(np.cos(angc), BF16), jnp.asarray(-np.sin(angc), BF16))


def _trunk(x, mods, lw, final_g):
    bsz, s, d = x.shape
    r = s // TOK
    halves = TOK // r
    assert r * TOK == s and r * halves == TOK and halves in (1, 2) and r % SUBLANES == 0
    tn = 8
    tk1 = 512 // r
    gw = d // N_FOURIER_GROUPS
    fs1, tw, m2, cc, cs = _dft_tables(r, tn, gw)
    depth = len(lw)
    for l in range(depth):
        w = lw[l]
        mod = mods[l]
        res = _lru_fwd(x, mod, w["norm_g"], w["w_xa"], w["conv_w"], w["conv_b"], w["wg"][0], w["bg"][0],
                       w["lam"][0], natural_in=(l == 0))
        xc, hf, hbf = res[:3]
        if l == 0:
            x = res[3]
        ya = _lru_bwd(xc, hf, hbf, w["w_ga"], w["wg"][1], w["bg"][1], w["lam"][1], w["w_a_out"])
        a = _fft_s1(x, mod, w["norm_g"], w["w_xb"], fs1, tw, tn=tn, halves=halves)
        x = _fft_s2(a, x, ya, mod, w["norm_g"], w["w_gm"], m2, cc, cs, w["w_b_out"], w["w_o"], final_g,
                    tk1=tk1, final=(l == depth - 1))
    return x.reshape(bsz, s, d)


def _layer_weights(norm_g, w_in, conv_w, conv_b, w_rg, b_rg, lam, w_a_out, w_b_out, w_o):
    depth, d = norm_g.shape
    lw = []
    for l in range(depth):
        wi = w_in[l].astype(BF16)
        lw.append(dict(
            norm_g=norm_g[l].reshape(1, d),
            w_xa=wi[:, 0:d], w_ga=wi[:, d:2 * d], w_xb=wi[:, 2 * d:3 * d], w_gm=wi[:, 3 * d:6 * d],
            conv_w=conv_w[l], conv_b=conv_b[l].reshape(1, d),
            wg=[jnp.concatenate([w_rg[l, dr, 0], w_rg[l, dr, 1]], axis=-1).astype(BF16) for dr in range(2)],
            bg=[b_rg[l, dr] for dr in range(2)],
            lam=[lam[l, dr].reshape(1, d) for dr in range(2)],
            w_a_out=w_a_out[l].astype(BF16), w_b_out=w_b_out[l].astype(BF16), w_o=w_o[l].astype(BF16)))
    return lw


def kernel(x_prompt, x_sample, c_prompt, c_sample, norm_g, w_ada, b_ada, w_in, conv_w, conv_b, w_rg, b_rg, lam,
           w_a_out, w_b_out, w_o, final_g):
    d = x_prompt.shape[-1]
    depth = norm_g.shape[0]
    bp, bs = c_prompt.shape[0], c_sample.shape[0]
    rows = -(-(bp + bs) // SUBLANES) * SUBLANES
    c_all = jnp.concatenate([c_prompt, c_sample, jnp.zeros((rows - bp - bs, d), F32)], axis=0)
    mod = _adaln_mod(c_all, w_ada, b_ada)
    mods_p = [mod[l, 0:bp].reshape(bp, 1, 3 * d) for l in range(depth)]
    mods_s = [mod[l, bp:bp + bs].reshape(bs, 1, 3 * d) for l in range(depth)]
    lw = _layer_weights(norm_g, w_in, conv_w, conv_b, w_rg, b_rg, lam, w_a_out, w_b_out, w_o)
    fg = final_g.reshape(1, d)
    return (_trunk(x_prompt, mods_p, lw, fg), _trunk(x_sample, mods_s, lw, fg))
```

```python
import functools

import numpy as np
import jax
import jax.numpy as jnp
from jax import lax
from jax.experimental import pallas as pl
from jax.experimental.pallas import tpu as pltpu

F32 = jnp.float32
BF16 = jnp.bfloat16

N_LRU_HEADS = 8
N_FOURIER_GROUPS = 4
CONV_WIDTH = 4
CONV_LEFT = 2
LRU_C = 8.0
EPS = 1e-6

SUBLANES = 8
TOK = 128
ROW_CHUNK = 512
VMEM_LIMIT = 58 * 1024 * 1024
TINY = 1e-30
LOG2E = 1.4426950408889634


def _cparams(n_axes):
    return pltpu.CompilerParams(dimension_semantics=("arbitrary",) * n_axes,
                                vmem_limit_bytes=VMEM_LIMIT)


def _resident(shape, index_map):
    return pl.BlockSpec(shape, index_map, pipeline_mode=pl.Buffered(1))


def _sigmoid(z):
    return 0.5 * jnp.tanh(0.5 * z) + 0.5


def _silu(z):
    zh = 0.5 * z
    return zh * (jnp.tanh(zh) + 1.0)


def _softplus(z):
    return jnp.maximum(z, 0.0) + jnp.log1p(jnp.exp(-jnp.abs(z)))


def _norm_mod(x, g1s, shift):
    ms = jnp.mean(x * x, axis=-1, keepdims=True)
    return x * lax.rsqrt(ms + EPS) * g1s + shift


def _mod_rows(mod_ref, d):
    return mod_ref[:, 0:d], mod_ref[:, d:2 * d], mod_ref[:, 2 * d:3 * d]


def _norm_to_rows(x_ref, hbf_ref, hs_ref, g1s, shift, n_groups, xs_ref=None):
    d = g1s.shape[1]
    g = SUBLANES

    def group(l):
        if xs_ref is None:
            return x_ref[:, l * d:(l + 1) * d]
        xg = x_ref[:, l, :]
        xs_ref[:, l * d:(l + 1) * d] = xg
        return xg

    for l in range(0, n_groups, 2):
        hn = _norm_mod(jnp.concatenate([group(l), group(l + 1)], axis=0), g1s, shift)
        hbf_ref[l * g:(l + 2) * g, :] = hn.astype(BF16)
        hs_ref[:, l * d:(l + 1) * d] = hn[:g]
        hs_ref[:, (l + 1) * d:(l + 2) * d] = hn[g:]


def _mod_kernel(c_ref, w_ref, b_ref, o_ref):
    c = c_ref[...]
    o_ref[...] = jnp.dot(_silu(c), w_ref[...], preferred_element_type=F32,
                         precision=lax.Precision.HIGHEST) + b_ref[...]


def _adaln_mod(c_all, w_ada, b_ada):
    depth, d, d3 = w_ada.shape
    rows = c_all.shape[0]
    return pl.pallas_call(
        _mod_kernel,
        out_shape=jax.ShapeDtypeStruct((depth, rows, d3), F32),
        grid=(depth, d3 // d),
        in_specs=[pl.BlockSpec((rows, d), lambda l, j: (0, 0)),
                  pl.BlockSpec((None, d, d), lambda l, j: (l, 0, j)),
                  pl.BlockSpec((None, 1, d), lambda l, j: (l, 0, j))],
        out_specs=pl.BlockSpec((None, rows, d), lambda l, j: (l, 0, j)),
        compiler_params=_cparams(2),
        name="adaln_mod",
    )(c_all, w_ada, b_ada.reshape(depth, 1, d3))


def _lru_coeffs(xh, gh, k2):
    hb = xh.shape[1]
    t_r = jnp.tanh(gh[:, :hb])
    t_i = jnp.tanh(gh[:, hb:])
    a = jnp.exp2((t_r + 1.0) * k2)
    y = 1.0 - a * a
    mult = y * lax.rsqrt(jnp.maximum(y, TINY))
    u = (t_i + 1.0) * xh
    return a, mult * u, u


def _decay_const(lam_ref):
    return (-0.5 * LRU_C * LOG2E) * _softplus(-lam_ref[...])


def _gate_dot(xhb, ones, wg):
    return jnp.dot(jnp.concatenate([xhb, ones], axis=1), wg, preferred_element_type=F32)


def _ones_column(n_rows, width):
    lane = lax.broadcasted_iota(jnp.int32, (n_rows, width), 1)
    return jnp.where(lane == 0, 1.0, 0.0).astype(BF16)


def _segment_ends(a_ref, b_ref, reverse):
    g = SUBLANES
    n_groups = a_ref.shape[0] // g
    d = a_ref.shape[1]

    def body(s, hp):
        l = (n_groups - 1 - s) if reverse else s
        r0 = pl.multiple_of(l * g, g)
        a = a_ref[pl.ds(r0, g), :]
        return (a * hp[0] + b_ref[pl.ds(r0, g), :], a * hp[1])

    return lax.fori_loop(0, n_groups, body, (jnp.zeros((g, d), F32), jnp.ones((g, d), F32)), unroll=8)


def _segment_apply(a_ref, b_ref, c_in, out_ref, reverse):
    g = SUBLANES
    n_pairs = a_ref.shape[0] // (2 * g)

    def body(s, h):
        p = (n_pairs - 1 - s) if reverse else s
        r0 = pl.multiple_of(p * 2 * g, 2 * g)
        a = a_ref[pl.ds(r0, 2 * g), :]
        b = b_ref[pl.ds(r0, 2 * g), :]
        if reverse:
            h1 = a[g:] * h + b[g:]
            h0 = a[:g] * h1 + b[:g]
            h = h0
        else:
            h0 = a[:g] * h + b[:g]
            h1 = a[g:] * h0 + b[g:]
            h = h1
        out_ref[pl.ds(r0, 2 * g), :] = jnp.concatenate([h0, h1], axis=0).astype(out_ref.dtype)
        return h

    lax.fori_loop(0, n_pairs, body, c_in, unroll=4)


def _segment_carries(h_end, p_end, carry, reverse):
    rows = [None] * SUBLANES
    c = carry
    for s in (range(SUBLANES - 1, -1, -1) if reverse else range(SUBLANES)):
        rows[s] = c
        c = h_end[s:s + 1, :] + p_end[s:s + 1, :] * c
    return jnp.concatenate(rows, axis=0), c


def _lru_fwd_kernel(*refs, n_tiles, natural_in):
    (x_ref, xn_ref, mod_ref, g_ref, wxa_ref, cw_ref, cb_ref, wg_ref, lam_ref,
     xc_ref, hf_ref, hbf_ref, hs_ref) = refs[:13]
    xs_ref = refs[13] if natural_in else None
    ext_ref, a_ref, b_ref, carry_ref, prev_ref, u_ref = refs[-6:]
    i = pl.program_id(1)
    g = SUBLANES
    d = g_ref.shape[1]
    n_rows = a_ref.shape[0]
    n_groups = n_rows // g
    hb = d // N_LRU_HEADS
    shift, scale, _ = _mod_rows(mod_ref, d)
    g1s = g_ref[...] * (1.0 + scale)

    @pl.when(i == 0)
    def _():
        carry_ref[...] = jnp.zeros_like(carry_ref)
        prev_ref[...] = jnp.zeros_like(prev_ref)

    _norm_to_rows(x_ref, hbf_ref, hs_ref, g1s, shift, n_groups, xs_ref)
    xn = xn_ref[:, 0, :] if natural_in else xn_ref[...]
    hbf_ref[n_rows:n_rows + 2 * g, :] = _norm_mod(jnp.concatenate([xn, xn], axis=0), g1s, shift).astype(BF16)
    ext_ref[CONV_LEFT * g:, :] = jnp.dot(hbf_ref[...], wxa_ref[...], preferred_element_type=F32)

    row = lax.broadcasted_iota(jnp.int32, (g, d), 0)
    last1 = ext_ref[(n_groups + 1) * g:(n_groups + 2) * g, :]
    last2 = ext_ref[n_groups * g:(n_groups + 1) * g, :]
    first = ext_ref[CONV_LEFT * g:(CONV_LEFT + 1) * g, :]
    nxt = ext_ref[(n_groups + 2) * g:(n_groups + 3) * g, :]
    ext_ref[g:2 * g, :] = jnp.where(row == 0, pltpu.roll(prev_ref[g:2 * g, :], 1, axis=0),
                                    pltpu.roll(last1, 1, axis=0))
    ext_ref[0:g, :] = jnp.where(row == 0, pltpu.roll(prev_ref[0:g, :], 1, axis=0),
                                pltpu.roll(last2, 1, axis=0))
    prev_ref[0:g, :] = last2
    prev_ref[g:2 * g, :] = last1
    nxt_scale = jnp.where(i == n_tiles - 1, 0.0, 1.0)
    ext_ref[(n_groups + 2) * g:(n_groups + 3) * g, :] = jnp.where(
        row == g - 1, pltpu.roll(nxt, g - 1, axis=0) * nxt_scale, pltpu.roll(first, g - 1, axis=0))

    k2 = _decay_const(lam_ref)
    ones = _ones_column(n_rows, hb)
    hcw = 0.5 * cw_ref[...]
    hcb = 0.5 * cb_ref[...]
    for h in range(N_LRU_HEADS):
        sl = slice(h * hb, (h + 1) * hb)
        xh = ext_ref[0:n_rows, sl] * hcw[0:1, sl] + hcb[:, sl]
        for kk in range(1, CONV_WIDTH):
            xh = xh + ext_ref[kk * g:kk * g + n_rows, sl] * hcw[kk:kk + 1, sl]
        xhb = xh.astype(BF16)
        xc_ref[:, sl] = xhb
        a, b, u = _lru_coeffs(xh, _gate_dot(xhb, ones, wg_ref[h]), k2[:, sl])
        a_ref[:, sl] = a
        b_ref[:, sl] = b
        u_ref[:, sl] = u[0:g, :]

    @pl.when(i == 0)
    def _():
        b_ref[0:g, :] = jnp.where(row == 0, u_ref[...], b_ref[0:g, :])

    h_end, p_end = _segment_ends(a_ref, b_ref, False)
    c_in, carry_ref[...] = _segment_carries(h_end, p_end, carry_ref[...], False)
    _segment_apply(a_ref, b_ref, c_in, hf_ref, False)


def _lru_fwd(x, mod, norm_g, w_xa, conv_w, conv_b, wg, lam, *, natural_in):
    g = SUBLANES
    if natural_in:
        bsz, s, d = x.shape
        r = s // TOK
        x = x.reshape(bsz, r, TOK, d)
        x_spec = pl.BlockSpec((None, g, TOK, d), lambda b, i: (b, i, 0, 0))
        xn_spec = pl.BlockSpec((None, g, g, d), lambda b, i: (b, jnp.minimum(i + 1, n_tiles - 1), 0, 0))
    else:
        bsz, r, ld = x.shape
        d = ld // TOK
        x_spec = pl.BlockSpec((None, g, TOK * d), lambda b, i: (b, i, 0))
        xn_spec = pl.BlockSpec((None, g, d), lambda b, i: (b, jnp.minimum(i + 1, n_tiles - 1), 0))
    n_tiles = r // g
    n_rows = TOK * g
    hb = d // N_LRU_HEADS
    const = lambda b, i: (0, 0)
    rowform = jax.ShapeDtypeStruct((bsz, n_tiles, n_rows, d), BF16)
    rowspec = pl.BlockSpec((None, None, n_rows, d), lambda b, i: (b, i, 0, 0))
    storage = jax.ShapeDtypeStruct((bsz, r, TOK * d), F32)
    storage_spec = pl.BlockSpec((None, g, TOK * d), lambda b, i: (b, i, 0))
    out_shape = [rowform, rowform, jax.ShapeDtypeStruct((bsz, n_tiles, n_rows + 2 * g, d), BF16), storage]
    out_specs = [rowspec, rowspec, pl.BlockSpec((None, None, n_rows + 2 * g, d), lambda b, i: (b, i, 0, 0)),
                 storage_spec]
    if natural_in:
        out_shape.append(storage)
        out_specs.append(storage_spec)
    return pl.pallas_call(
        functools.partial(_lru_fwd_kernel, n_tiles=n_tiles, natural_in=natural_in),
        out_shape=tuple(out_shape),
        grid=(bsz, n_tiles),
        in_specs=[x_spec, xn_spec,
                  pl.BlockSpec((None, 1, 3 * d), lambda b, i: (b, 0, 0)),
                  _resident((1, d), const),
                  _resident((d, d), const),
                  _resident((CONV_WIDTH, d), const),
                  _resident((1, d), const),
                  _resident((N_LRU_HEADS, 2 * hb, 2 * hb), lambda b, i: (0, 0, 0)),
                  _resident((1, d), const)],
        out_specs=tuple(out_specs),
        scratch_shapes=[pltpu.VMEM((n_rows + (CONV_LEFT + 2) * g, d), F32),
                        pltpu.VMEM((n_rows, d), F32),
                        pltpu.VMEM((n_rows, d), F32),
                        pltpu.VMEM((1, d), F32),
                        pltpu.VMEM((CONV_LEFT * g, d), F32),
                        pltpu.VMEM((g, d), F32)],
        compiler_params=_cparams(2),
        name="lru_fwd",
    )(x, x, mod, norm_g, w_xa, conv_w, conv_b, wg, lam)


def _lru_bwd_kernel(xc_ref, hf_ref, hbf_ref, wga_ref, wg_ref, lam_ref, wao_ref,
                    ya_ref, a_ref, b_ref, carry_ref, u_ref):
    i = pl.program_id(1)
    g = SUBLANES
    n_rows, d = a_ref.shape
    hb = d // N_LRU_HEADS

    @pl.when(i == 0)
    def _():
        carry_ref[...] = jnp.zeros_like(carry_ref)

    k2 = _decay_const(lam_ref)
    ones = _ones_column(n_rows, hb)
    for h in range(N_LRU_HEADS):
        sl = slice(h * hb, (h + 1) * hb)
        xhb = xc_ref[:, sl]
        a, b, u = _lru_coeffs(xhb.astype(F32), _gate_dot(xhb, ones, wg_ref[h]), k2[:, sl])
        a_ref[:, sl] = a
        b_ref[:, sl] = b
        u_ref[:, sl] = u[n_rows - g:, :]

    @pl.when(i == 0)
    def _():
        row = lax.broadcasted_iota(jnp.int32, (g, d), 0)
        b_ref[n_rows - g:, :] = jnp.where(row == g - 1, u_ref[...], b_ref[n_rows - g:, :])

    h_end, p_end = _segment_ends(a_ref, b_ref, True)
    c_in, carry_ref[...] = _segment_carries(h_end, p_end, carry_ref[...], True)
    _segment_apply(a_ref, b_ref, c_in, b_ref, True)

    for r0 in range(0, n_rows, ROW_CHUNK):
        rs = slice(r0, r0 + ROW_CHUNK)
        ga = jnp.dot(hbf_ref[rs, :], wga_ref[...], preferred_element_type=F32)
        ya = (hf_ref[rs, :].astype(F32) + b_ref[rs, :]) * _silu(ga)
        out = jnp.dot(ya.astype(BF16), wao_ref[...], preferred_element_type=F32)
        for gl in range(ROW_CHUNK // g):
            l = r0 // g + gl
            ya_ref[:, l * d:(l + 1) * d] = out[gl * g:(gl + 1) * g, :]


def _lru_bwd(xc, hf, hbf, w_ga, wg, lam, w_a_out):
    bsz, n_tiles, n_rows, d = xc.shape
    g = SUBLANES
    hb = d // N_LRU_HEADS
    const = lambda b, i: (0, 0)
    rev3 = lambda b, i: (b, n_tiles - 1 - i, 0)
    rev4 = lambda b, i: (b, n_tiles - 1 - i, 0, 0)
    return pl.pallas_call(
        _lru_bwd_kernel,
        out_shape=jax.ShapeDtypeStruct((bsz, n_tiles * g, TOK * d), F32),
        grid=(bsz, n_tiles),
        in_specs=[pl.BlockSpec((None, None, n_rows, d), rev4),
                  pl.BlockSpec((None, None, n_rows, d), rev4),
                  pl.BlockSpec((None, None, n_rows + 2 * g, d), rev4),
                  _resident((d, d), const),
                  _resident((N_LRU_HEADS, 2 * hb, 2 * hb), lambda b, i: (0, 0, 0)),
                  _resident((1, d), const),
                  _resident((d, d), const)],
        out_specs=pl.BlockSpec((None, g, TOK * d), rev3),
        scratch_shapes=[pltpu.VMEM((n_rows, d), F32),
                        pltpu.VMEM((n_rows, d), F32),
                        pltpu.VMEM((1, d), F32),
                        pltpu.VMEM((g, d), F32)],
        compiler_params=_cparams(2),
        name="lru_bwd",
    )(xc, hf, hbf, w_ga, wg, lam, w_a_out)


def _pack_complex(re, im):
    hi = lax.bitcast_convert_type(re.astype(BF16).astype(F32), jnp.uint32)
    lo = lax.bitcast_convert_type(im.astype(BF16).astype(F32), jnp.uint32)
    return hi | (lo >> 16)


def _unpack_complex(w):
    re = lax.bitcast_convert_type(w & jnp.uint32(0xFFFF0000), F32)
    im = lax.bitcast_convert_type(w << 16, F32)
    return re.astype(BF16), im.astype(BF16)


def _fft_s1_kernel(*refs, tn, halves):
    x_refs = refs[:halves]
    tw_refs = refs[halves:2 * halves]
    wxb_ref, fs1_ref, o_ref = refs[2 * halves:]
    r = x_refs[0].shape[0]
    d = wxb_ref.shape[0]
    hn_all = jnp.concatenate(
        [x_refs[hf][:, j * d:(j + 1) * d].astype(BF16) for hf in range(halves) for j in range(tn)], axis=0)
    xb_all = jnp.dot(hn_all, wxb_ref[...], preferred_element_type=F32).astype(BF16)
    for j in range(tn):
        parts = []
        for hf in range(halves):
            c0 = (hf * tn + j) * r
            pq = jnp.dot(fs1_ref[...], xb_all[c0:c0 + r, :], preferred_element_type=F32)
            p, q = pq[:r], pq[r:]
            tc = tw_refs[hf][:, j:j + 1]
            ts = tw_refs[hf][:, tn + j:tn + j + 1]
            parts.append((p * tc - q * ts, p * ts + q * tc))
        if halves == 1:
            o_ref[0, :, j, :] = _pack_complex(*parts[0])
        else:
            o_ref[0, :, j, :] = _pack_complex(parts[0][0] + parts[1][0], parts[0][1] + parts[1][1])
            o_ref[1, :, j, :] = _pack_complex(parts[0][0] - parts[1][0], parts[0][1] - parts[1][1])


def _fft_s1(x, w_xb, fs1, tw, *, tn, halves):
    bsz, r, ld = x.shape
    d = ld // TOK
    nc = TOK // halves
    nblk = nc // tn
    const = lambda b, i: (0, 0)
    x_specs = [pl.BlockSpec((None, r, tn * d), functools.partial(lambda b, i, hf: (b, 0, hf * nblk + i), hf=hf))
               for hf in range(halves)]
    tw_specs = [pl.BlockSpec((None, r, 2 * tn), functools.partial(lambda b, i, hf: (hf * nblk + i, 0, 0), hf=hf))
                for hf in range(halves)]
    return pl.pallas_call(
        functools.partial(_fft_s1_kernel, tn=tn, halves=halves),
        out_shape=jax.ShapeDtypeStruct((bsz, halves, r, nc, d), jnp.uint32),
        grid=(bsz, nblk),
        in_specs=x_specs + tw_specs + [
            _resident((d, d), const),
            _resident((2 * r, r), const)],
        out_specs=pl.BlockSpec((None, halves, r, tn, d), lambda b, i: (b, 0, 0, i, 0)),
        compiler_params=_cparams(2),
        name="fft_s1",
    )(*([x] * halves), *([tw] * halves), w_xb, fs1)


def _fft_s2_kernel(a_ref, x_ref, h_ref, ya_ref, mod_ref, wgm_ref, m2_ref, cc_ref, cs_ref, wbo_ref, wo_ref,
                   fg_ref, o_ref, *, tk1, final):
    r = x_ref.shape[0]
    d = fg_ref.shape[1]
    gw = d // N_FOURIER_GROUPS
    _, _, gate = _mod_rows(mod_ref, d)
    cols = [slice(q * d, (q + 1) * d) for q in range(tk1)]

    us = [jnp.dot(m2_ref[...], jnp.concatenate(_unpack_complex(a_ref[q]), axis=0),
                  preferred_element_type=F32) for q in range(tk1)]
    ur = jnp.concatenate([u[:r] for u in us], axis=0).astype(BF16)
    uq = jnp.concatenate([u[r:] for u in us], axis=0).astype(BF16)
    x = jnp.concatenate([x_ref[:, cs] for cs in cols], axis=0)
    ya = jnp.concatenate([ya_ref[:, cs] for cs in cols], axis=0)

    hn = jnp.concatenate([h_ref[:, cs] for cs in cols], axis=0).astype(BF16)
    gm = jnp.dot(hn, wgm_ref[...], preferred_element_type=F32)
    y = jnp.concatenate(
        [jnp.dot(ur[:, gi * gw:(gi + 1) * gw], cc_ref[...], preferred_element_type=F32)
         + jnp.dot(uq[:, gi * gw:(gi + 1) * gw], cs_ref[...], preferred_element_type=F32)
         for gi in range(N_FOURIER_GROUPS)], axis=1)
    yb = jnp.dot((y * _silu(gm[:, :d])).astype(BF16), wbo_ref[...], preferred_element_type=F32)
    mix = _sigmoid(gm[:, d:2 * d]) * ya + _sigmoid(gm[:, 2 * d:]) * yb
    out = jnp.dot(mix.astype(BF16), wo_ref[...], preferred_element_type=F32)
    xn = x + gate * out
    if final:
        ms = jnp.mean(xn * xn, axis=-1, keepdims=True)
        xn = xn * lax.rsqrt(ms + EPS) * fg_ref[...]
    for q, cs in enumerate(cols):
        if len(o_ref.shape) == 3:
            o_ref[:, q, :] = xn[q * r:(q + 1) * r, :]
        else:
            o_ref[:, cs] = xn[q * r:(q + 1) * r, :]


def _fft_s2(a, x, h, ya, mod, w_gm, m2, cc, cs, w_b_out, w_o, final_g, *, tk1, final):
    bsz, r, ld = x.shape
    d = ld // TOK
    halves = a.shape[1]
    nc = a.shape[3]
    nblk = (TOK // halves) // tk1
    const = lambda b, k2, i: (0, 0)
    tok = pl.BlockSpec((None, r, tk1 * d), lambda b, k2, i: (b, 0, k2 * nblk + i))
    gw = d // N_FOURIER_GROUPS
    if final and tk1 % SUBLANES == 0:
        out_shape = jax.ShapeDtypeStruct((bsz, r, TOK, d), F32)
        out_spec = pl.BlockSpec((None, r, tk1, d), lambda b, k2, i: (b, 0, k2 * nblk + i, 0))
    else:
        out_shape = jax.ShapeDtypeStruct((bsz, r, ld), F32)
        out_spec = tok
    return pl.pallas_call(
        functools.partial(_fft_s2_kernel, tk1=tk1, final=final),
        out_shape=out_shape,
        grid=(bsz, halves, nblk),
        in_specs=[pl.BlockSpec((None, None, tk1, nc, d), lambda b, k2, i: (b, k2, i, 0, 0)),
                  tok, tok, tok,
                  pl.BlockSpec((None, 1, 3 * d), lambda b, k2, i: (b, 0, 0)),
                  _resident((d, 3 * d), const),
                  pl.BlockSpec((None, 2 * nc, 2 * nc), lambda b, k2, i: (k2, 0, 0)),
                  _resident((gw, gw), const),
                  _resident((gw, gw), const),
                  _resident((d, d), const),
                  _resident((d, d), const),
                  _resident((1, d), const)],
        out_specs=out_spec,
        compiler_params=_cparams(3),
        name="fft_s2",
    )(a, x, h, ya, mod, w_gm, m2, cc, cs, w_b_out, w_o, final_g)


def _dft_tables(r, tn, gw):
    halves = TOK // r
    s = r * TOK
    k1 = np.arange(r)
    ang1 = 2.0 * np.pi * np.outer(k1, k1) / r
    fs1 = np.concatenate([np.cos(ang1), np.sin(ang1)], axis=0)
    tok = np.arange(TOK)
    ang_t = 2.0 * np.pi * np.outer(k1, tok) / s
    scale = 1.0 / np.sqrt(float(s) * gw)
    tc = (np.cos(ang_t) * scale).reshape(r, TOK // tn, tn).transpose(1, 0, 2)
    ts = (np.sin(ang_t) * scale).reshape(r, TOK // tn, tn).transpose(1, 0, 2)
    tw = np.concatenate([tc, ts], axis=2)
    n = np.arange(r)
    m2 = []
    for k2 in range(halves):
        ang2 = 2.0 * np.pi * np.outer(halves * n + k2, n) / (halves * r)
        c2, s2 = np.cos(ang2), np.sin(ang2)
        m2.append(np.block([[c2, -s2], [s2, c2]]))
    c = np.arange(gw)
    angc = 2.0 * np.pi * np.outer(c, c) / gw
    return (jnp.asarray(fs1, BF16), jnp.asarray(tw, F32), jnp.asarray(np.stack(m2), BF16),
            jnp.asarray(np.cos(angc), BF16), jnp.asarray(-np.sin(angc), BF16))


def _trunk(x, mods, lw, final_g):
    bsz, s, d = x.shape
    r = s // TOK
    halves = TOK // r
    assert r * TOK == s and r * halves == TOK and halves in (1, 2) and r % SUBLANES == 0
    tn = 8
    tk1 = 512 // r
    gw = d // N_FOURIER_GROUPS
    fs1, tw, m2, cc, cs = _dft_tables(r, tn, gw)
    depth = len(lw)
    for l in range(depth):
        w = lw[l]
        mod = mods[l]
        res = _lru_fwd(x, mod, w["norm_g"], w["w_xa"], w["conv_w"], w["conv_b"], w["wg"][0], w["lam"][0],
                       natural_in=(l == 0))
        xc, hf, hbf, h = res[:4]
        if l == 0:
            x = res[4]
        ya = _lru_bwd(xc, hf, hbf, w["w_ga"], w["wg"][1], w["lam"][1], w["w_a_out"])
        a = _fft_s1(h, w["w_xb"], fs1, tw, tn=tn, halves=halves)
        x = _fft_s2(a, x, h, ya, mod, w["w_gm"], m2, cc, cs, w["w_b_out"], w["w_o"], final_g,
                    tk1=tk1, final=(l == depth - 1))
    return x.reshape(bsz, s, d)


def _gate_weights(w, b):
    n_heads, blk = w.shape[1], w.shape[2]
    bias = 0.5 * jnp.concatenate([b[0].reshape(n_heads, 1, blk), b[1].reshape(n_heads, 1, blk)], axis=-1)
    rows = jnp.concatenate([jnp.concatenate([w[0], w[1]], axis=-1), bias,
                            jnp.zeros((n_heads, blk - 1, 2 * blk), F32)], axis=1)
    return rows.astype(BF16)


def _layer_weights(norm_g, w_in, conv_w, conv_b, w_rg, b_rg, lam, w_a_out, w_b_out, w_o):
    depth, d = norm_g.shape
    lw = []
    for l in range(depth):
        wi = w_in[l].astype(BF16)
        lw.append(dict(
            norm_g=norm_g[l].reshape(1, d),
            w_xa=wi[:, 0:d], w_ga=wi[:, d:2 * d], w_xb=wi[:, 2 * d:3 * d], w_gm=wi[:, 3 * d:6 * d],
            conv_w=conv_w[l], conv_b=conv_b[l].reshape(1, d),
            wg=[_gate_weights(w_rg[l, dr], b_rg[l, dr]) for dr in range(2)],
            lam=[lam[l, dr].reshape(1, d) for dr in range(2)],
            w_a_out=w_a_out[l].astype(BF16), w_b_out=w_b_out[l].astype(BF16), w_o=w_o[l].astype(BF16)))
    return lw


def kernel(x_prompt, x_sample, c_prompt, c_sample, norm_g, w_ada, b_ada, w_in, conv_w, conv_b, w_rg, b_rg, lam,
           w_a_out, w_b_out, w_o, final_g):
    d = x_prompt.shape[-1]
    depth = norm_g.shape[0]
    bp, bs = c_prompt.shape[0], c_sample.shape[0]
    rows = -(-(bp + bs) // SUBLANES) * SUBLANES
    c_all = jnp.concatenate([c_prompt, c_sample, jnp.zeros((rows - bp - bs, d), F32)], axis=0)
    mod = _adaln_mod(c_all, w_ada, b_ada)
    mods_p = [mod[l, 0:bp].reshape(bp, 1, 3 * d) for l in range(depth)]
    mods_s = [mod[l, bp:bp + bs].reshape(bs, 1, 3 * d) for l in range(depth)]
    lw = _layer_weights(norm_g, w_in, conv_w, conv_b, w_rg, b_rg, lam, w_a_out, w_b_out, w_o)
    fg = final_g.reshape(1, d)
    return (_trunk(x_prompt, mods_p, lw, fg), _trunk(x_sample, mods_s, lw, fg))
```

```python
import functools

import numpy as np
import jax
import jax.numpy as jnp
from jax import lax
from jax.experimental import pallas as pl
from jax.experimental.pallas import tpu as pltpu

F32 = jnp.float32
BF16 = jnp.bfloat16

N_LRU_HEADS = 8
N_FOURIER_GROUPS = 4
CONV_WIDTH = 4
CONV_LEFT = 2
LRU_C = 8.0
EPS = 1e-6

SUBLANES = 8
TOK = 128
ROW_CHUNK = 512
XA_COLS = 256
VMEM_LIMIT = 58 * 1024 * 1024
TINY = 1e-30
LOG2E = 1.4426950408889634


def _cparams(n_axes):
    return pltpu.CompilerParams(dimension_semantics=("arbitrary",) * n_axes,
                                vmem_limit_bytes=VMEM_LIMIT)


def _resident(shape, index_map):
    return pl.BlockSpec(shape, index_map, pipeline_mode=pl.Buffered(1))


def _sigmoid(z):
    return 0.5 * jnp.tanh(0.5 * z) + 0.5


def _silu(z):
    zh = 0.5 * z
    return zh * (jnp.tanh(zh) + 1.0)


def _softplus(z):
    return jnp.maximum(z, 0.0) + jnp.log1p(jnp.exp(-jnp.abs(z)))


def _norm_mod(x, g1s, shift):
    ms = jnp.mean(x * x, axis=-1, keepdims=True)
    return x * lax.rsqrt(ms + EPS) * g1s + shift


def _mod_rows(mod_ref, d):
    return mod_ref[:, 0:d], mod_ref[:, d:2 * d], mod_ref[:, 2 * d:3 * d]


def _norm_to_rows(x_ref, hbf_ref, hs_ref, g1s, shift, n_groups, xs_ref=None):
    d = g1s.shape[1]
    g = SUBLANES

    def group(l):
        if xs_ref is None:
            return x_ref[:, l * d:(l + 1) * d]
        xg = x_ref[:, l, :]
        xs_ref[:, l * d:(l + 1) * d] = xg
        return xg

    for l in range(0, n_groups, 2):
        hn = _norm_mod(jnp.concatenate([group(l), group(l + 1)], axis=0), g1s, shift)
        hbf_ref[l * g:(l + 2) * g, :] = hn.astype(BF16)
        hs_ref[:, l * d:(l + 1) * d] = hn[:g]
        hs_ref[:, (l + 1) * d:(l + 2) * d] = hn[g:]


def _mod_kernel(c_ref, w_ref, b_ref, o_ref):
    c = c_ref[...]
    o_ref[...] = jnp.dot(_silu(c), w_ref[...], preferred_element_type=F32,
                         precision=lax.Precision.HIGHEST) + b_ref[...]


def _adaln_mod(c_all, w_ada, b_ada):
    depth, d, d3 = w_ada.shape
    rows = c_all.shape[0]
    return pl.pallas_call(
        _mod_kernel,
        out_shape=jax.ShapeDtypeStruct((depth, rows, d3), F32),
        grid=(depth, d3 // d),
        in_specs=[pl.BlockSpec((rows, d), lambda l, j: (0, 0)),
                  pl.BlockSpec((None, d, d), lambda l, j: (l, 0, j)),
                  pl.BlockSpec((None, 1, d), lambda l, j: (l, 0, j))],
        out_specs=pl.BlockSpec((None, rows, d), lambda l, j: (l, 0, j)),
        compiler_params=_cparams(2),
        name="adaln_mod",
    )(c_all, w_ada, b_ada.reshape(depth, 1, d3))


def _lru_coeffs(xh, gh, k2):
    hb = xh.shape[1]
    t_r = jnp.tanh(gh[:, :hb])
    t_i = jnp.tanh(gh[:, hb:])
    a = jnp.exp2((t_r + 1.0) * k2)
    y = 1.0 - a * a
    mult = y * lax.rsqrt(jnp.maximum(y, TINY))
    u = (t_i + 1.0) * xh
    return a, mult * u, u


def _decay_const(lam_ref):
    return (-0.5 * LRU_C * LOG2E) * _softplus(-lam_ref[...])


def _gate_dot(xhb, ones, wg):
    return jnp.dot(jnp.concatenate([xhb, ones], axis=1), wg, preferred_element_type=F32)


def _ones_column(n_rows, width):
    lane = lax.broadcasted_iota(jnp.int32, (n_rows, width), 1)
    return jnp.where(lane == 0, 1.0, 0.0).astype(BF16)


def _segment_ends(a_ref, b_ref, reverse):
    g = SUBLANES
    n_groups = a_ref.shape[0] // g
    d = a_ref.shape[1]

    def body(s, hp):
        l = (n_groups - 1 - s) if reverse else s
        r0 = pl.multiple_of(l * g, g)
        a = a_ref[pl.ds(r0, g), :]
        return (a * hp[0] + b_ref[pl.ds(r0, g), :], a * hp[1])

    return lax.fori_loop(0, n_groups, body, (jnp.zeros((g, d), F32), jnp.ones((g, d), F32)), unroll=8)


def _segment_apply(a_ref, b_ref, c_in, out_ref, reverse):
    g = SUBLANES
    n_pairs = a_ref.shape[0] // (2 * g)

    def body(s, h):
        p = (n_pairs - 1 - s) if reverse else s
        r0 = pl.multiple_of(p * 2 * g, 2 * g)
        a = a_ref[pl.ds(r0, 2 * g), :]
        b = b_ref[pl.ds(r0, 2 * g), :]
        if reverse:
            h1 = a[g:] * h + b[g:]
            h0 = a[:g] * h1 + b[:g]
            h = h0
        else:
            h0 = a[:g] * h + b[:g]
            h1 = a[g:] * h0 + b[g:]
            h = h1
        out_ref[pl.ds(r0, 2 * g), :] = jnp.concatenate([h0, h1], axis=0).astype(out_ref.dtype)
        return h

    lax.fori_loop(0, n_pairs, body, c_in, unroll=4)


def _segment_carries(h_end, p_end, carry, reverse):
    rows = [None] * SUBLANES
    c = carry
    for s in (range(SUBLANES - 1, -1, -1) if reverse else range(SUBLANES)):
        rows[s] = c
        c = h_end[s:s + 1, :] + p_end[s:s + 1, :] * c
    return jnp.concatenate(rows, axis=0), c


def _lru_fwd_kernel(*refs, n_tiles, natural_in):
    (x_ref, xn_ref, mod_ref, g_ref, wxa_ref, cw_ref, cb_ref, wg_ref, lam_ref,
     xc_ref, hf_ref, hbf_ref, hs_ref) = refs[:13]
    xs_ref = refs[13] if natural_in else None
    ext_ref, a_ref, b_ref, carry_ref, prev_ref, u_ref = refs[-6:]
    i = pl.program_id(1)
    g = SUBLANES
    d = g_ref.shape[1]
    n_rows = a_ref.shape[0]
    n_groups = n_rows // g
    hb = d // N_LRU_HEADS
    shift, scale, _ = _mod_rows(mod_ref, d)
    g1s = g_ref[...] * (1.0 + scale)

    @pl.when(i == 0)
    def _():
        carry_ref[...] = jnp.zeros_like(carry_ref)
        prev_ref[...] = jnp.zeros_like(prev_ref)

    _norm_to_rows(x_ref, hbf_ref, hs_ref, g1s, shift, n_groups, xs_ref)
    xn = xn_ref[:, 0, :] if natural_in else xn_ref[...]
    hbf_ref[n_rows:n_rows + 2 * g, :] = _norm_mod(jnp.concatenate([xn, xn], axis=0), g1s, shift).astype(BF16)
    row = lax.broadcasted_iota(jnp.int32, (g, XA_COLS), 0)
    nxt_scale = jnp.where(i == n_tiles - 1, 0.0, 1.0)

    def xa_block(k):
        cs = slice(k * XA_COLS, (k + 1) * XA_COLS)
        ext_ref[CONV_LEFT * g:, cs] = jnp.dot(hbf_ref[...], wxa_ref[:, cs], preferred_element_type=F32)
        last1 = ext_ref[(n_groups + 1) * g:(n_groups + 2) * g, cs]
        last2 = ext_ref[n_groups * g:(n_groups + 1) * g, cs]
        first = ext_ref[CONV_LEFT * g:(CONV_LEFT + 1) * g, cs]
        nxt = ext_ref[(n_groups + 2) * g:(n_groups + 3) * g, cs]
        ext_ref[g:2 * g, cs] = jnp.where(row == 0, pltpu.roll(prev_ref[g:2 * g, cs], 1, axis=0),
                                         pltpu.roll(last1, 1, axis=0))
        ext_ref[0:g, cs] = jnp.where(row == 0, pltpu.roll(prev_ref[0:g, cs], 1, axis=0),
                                     pltpu.roll(last2, 1, axis=0))
        prev_ref[0:g, cs] = last2
        prev_ref[g:2 * g, cs] = last1
        ext_ref[(n_groups + 2) * g:(n_groups + 3) * g, cs] = jnp.where(
            row == g - 1, pltpu.roll(nxt, g - 1, axis=0) * nxt_scale, pltpu.roll(first, g - 1, axis=0))

    k2 = _decay_const(lam_ref)
    ones = _ones_column(n_rows, hb)
    hcw = 0.5 * cw_ref[...]
    hcb = 0.5 * cb_ref[...]
    heads_per_block = XA_COLS // hb
    xa_block(0)
    for h in range(N_LRU_HEADS):
        if h % heads_per_block == 0 and (h // heads_per_block + 1) * XA_COLS < d:
            xa_block(h // heads_per_block + 1)
        sl = slice(h * hb, (h + 1) * hb)
        xh = ext_ref[0:n_rows, sl] * hcw[0:1, sl] + hcb[:, sl]
        for kk in range(1, CONV_WIDTH):
            xh = xh + ext_ref[kk * g:kk * g + n_rows, sl] * hcw[kk:kk + 1, sl]
        xhb = xh.astype(BF16)
        xc_ref[:, sl] = xhb
        a, b, u = _lru_coeffs(xh, _gate_dot(xhb, ones, wg_ref[h]), k2[:, sl])
        a_ref[:, sl] = a
        b_ref[:, sl] = b
        u_ref[:, sl] = u[0:g, :]

    @pl.when(i == 0)
    def _():
        first_row = lax.broadcasted_iota(jnp.int32, (g, d), 0) == 0
        b_ref[0:g, :] = jnp.where(first_row, u_ref[...], b_ref[0:g, :])

    h_end, p_end = _segment_ends(a_ref, b_ref, False)
    c_in, carry_ref[...] = _segment_carries(h_end, p_end, carry_ref[...], False)
    _segment_apply(a_ref, b_ref, c_in, hf_ref, False)


def _lru_fwd(x, mod, norm_g, w_xa, conv_w, conv_b, wg, lam, *, natural_in):
    g = SUBLANES
    if natural_in:
        bsz, s, d = x.shape
        r = s // TOK
        x = x.reshape(bsz, r, TOK, d)
        x_spec = pl.BlockSpec((None, g, TOK, d), lambda b, i: (b, i, 0, 0))
        xn_spec = pl.BlockSpec((None, g, g, d), lambda b, i: (b, jnp.minimum(i + 1, n_tiles - 1), 0, 0))
    else:
        bsz, r, ld = x.shape
        d = ld // TOK
        x_spec = pl.BlockSpec((None, g, TOK * d), lambda b, i: (b, i, 0))
        xn_spec = pl.BlockSpec((None, g, d), lambda b, i: (b, jnp.minimum(i + 1, n_tiles - 1), 0))
    n_tiles = r // g
    n_rows = TOK * g
    hb = d // N_LRU_HEADS
    const = lambda b, i: (0, 0)
    rowform = jax.ShapeDtypeStruct((bsz, n_tiles, n_rows, d), BF16)
    rowspec = pl.BlockSpec((None, None, n_rows, d), lambda b, i: (b, i, 0, 0))
    storage = jax.ShapeDtypeStruct((bsz, r, TOK * d), F32)
    storage_spec = pl.BlockSpec((None, g, TOK * d), lambda b, i: (b, i, 0))
    out_shape = [rowform, rowform, jax.ShapeDtypeStruct((bsz, n_tiles, n_rows + 2 * g, d), BF16), storage]
    out_specs = [rowspec, rowspec, pl.BlockSpec((None, None, n_rows + 2 * g, d), lambda b, i: (b, i, 0, 0)),
                 storage_spec]
    if natural_in:
        out_shape.append(storage)
        out_specs.append(storage_spec)
    return pl.pallas_call(
        functools.partial(_lru_fwd_kernel, n_tiles=n_tiles, natural_in=natural_in),
        out_shape=tuple(out_shape),
        grid=(bsz, n_tiles),
        in_specs=[x_spec, xn_spec,
                  pl.BlockSpec((None, 1, 3 * d), lambda b, i: (b, 0, 0)),
                  _resident((1, d), const),
                  _resident((d, d), const),
                  _resident((CONV_WIDTH, d), const),
                  _resident((1, d), const),
                  _resident((N_LRU_HEADS, 2 * hb, 2 * hb), lambda b, i: (0, 0, 0)),
                  _resident((1, d), const)],
        out_specs=tuple(out_specs),
        scratch_shapes=[pltpu.VMEM((n_rows + (CONV_LEFT + 2) * g, d), F32),
                        pltpu.VMEM((n_rows, d), F32),
                        pltpu.VMEM((n_rows, d), F32),
                        pltpu.VMEM((1, d), F32),
                        pltpu.VMEM((CONV_LEFT * g, d), F32),
                        pltpu.VMEM((g, d), F32)],
        compiler_params=_cparams(2),
        name="lru_fwd",
    )(x, x, mod, norm_g, w_xa, conv_w, conv_b, wg, lam)


def _lru_bwd_kernel(xc_ref, hf_ref, hbf_ref, wga_ref, wg_ref, lam_ref, wao_ref,
                    ya_ref, a_ref, b_ref, sg_ref, carry_ref, u_ref):
    i = pl.program_id(1)
    g = SUBLANES
    n_rows, d = a_ref.shape
    hb = d // N_LRU_HEADS

    @pl.when(i == 0)
    def _():
        carry_ref[...] = jnp.zeros_like(carry_ref)

    k2 = _decay_const(lam_ref)
    ones = _ones_column(n_rows, hb)
    heads_per_block = XA_COLS // hb
    for h in range(N_LRU_HEADS):
        if h % heads_per_block == 0:
            cs = slice(h * hb, h * hb + XA_COLS)
            sg_ref[:, cs] = _silu(jnp.dot(hbf_ref[0:n_rows, :], wga_ref[:, cs], preferred_element_type=F32))
        sl = slice(h * hb, (h + 1) * hb)
        xhb = xc_ref[:, sl]
        a, b, u = _lru_coeffs(xhb.astype(F32), _gate_dot(xhb, ones, wg_ref[h]), k2[:, sl])
        a_ref[:, sl] = a
        b_ref[:, sl] = b
        u_ref[:, sl] = u[n_rows - g:, :]

    @pl.when(i == 0)
    def _():
        row = lax.broadcasted_iota(jnp.int32, (g, d), 0)
        b_ref[n_rows - g:, :] = jnp.where(row == g - 1, u_ref[...], b_ref[n_rows - g:, :])

    h_end, p_end = _segment_ends(a_ref, b_ref, True)
    c_in, carry_ref[...] = _segment_carries(h_end, p_end, carry_ref[...], True)
    _segment_apply(a_ref, b_ref, c_in, b_ref, True)

    for r0 in range(0, n_rows, ROW_CHUNK):
        rs = slice(r0, r0 + ROW_CHUNK)
        ya = (hf_ref[rs, :].astype(F32) + b_ref[rs, :]) * sg_ref[rs, :]
        out = jnp.dot(ya.astype(BF16), wao_ref[...], preferred_element_type=F32)
        for gl in range(ROW_CHUNK // g):
            l = r0 // g + gl
            ya_ref[:, l * d:(l + 1) * d] = out[gl * g:(gl + 1) * g, :]


def _lru_bwd(xc, hf, hbf, w_ga, wg, lam, w_a_out):
    bsz, n_tiles, n_rows, d = xc.shape
    g = SUBLANES
    hb = d // N_LRU_HEADS
    const = lambda b, i: (0, 0)
    rev3 = lambda b, i: (b, n_tiles - 1 - i, 0)
    rev4 = lambda b, i: (b, n_tiles - 1 - i, 0, 0)
    return pl.pallas_call(
        _lru_bwd_kernel,
        out_shape=jax.ShapeDtypeStruct((bsz, n_tiles * g, TOK * d), F32),
        grid=(bsz, n_tiles),
        in_specs=[pl.BlockSpec((None, None, n_rows, d), rev4),
                  pl.BlockSpec((None, None, n_rows, d), rev4),
                  pl.BlockSpec((None, None, n_rows + 2 * g, d), rev4),
                  _resident((d, d), const),
                  _resident((N_LRU_HEADS, 2 * hb, 2 * hb), lambda b, i: (0, 0, 0)),
                  _resident((1, d), const),
                  _resident((d, d), const)],
        out_specs=pl.BlockSpec((None, g, TOK * d), rev3),
        scratch_shapes=[pltpu.VMEM((n_rows, d), F32),
                        pltpu.VMEM((n_rows, d), F32),
                        pltpu.VMEM((n_rows, d), F32),
                        pltpu.VMEM((1, d), F32),
                        pltpu.VMEM((g, d), F32)],
        compiler_params=_cparams(2),
        name="lru_bwd",
    )(xc, hf, hbf, w_ga, wg, lam, w_a_out)


def _pack_complex(re, im):
    hi = lax.bitcast_convert_type(re.astype(BF16).astype(F32), jnp.uint32)
    lo = lax.bitcast_convert_type(im.astype(BF16).astype(F32), jnp.uint32)
    return hi | (lo >> 16)


def _unpack_complex(w):
    re = lax.bitcast_convert_type(w & jnp.uint32(0xFFFF0000), F32)
    im = lax.bitcast_convert_type(w << 16, F32)
    return re.astype(BF16), im.astype(BF16)


def _fft_s1_kernel(*refs, tn, halves):
    x_refs = refs[:halves]
    tw_refs = refs[halves:2 * halves]
    wxb_ref, fs1_ref, o_ref = refs[2 * halves:]
    r = x_refs[0].shape[0]
    d = wxb_ref.shape[0]
    hn_all = jnp.concatenate(
        [x_refs[hf][:, j * d:(j + 1) * d].astype(BF16) for hf in range(halves) for j in range(tn)], axis=0)
    def xb_block(k):
        return jnp.dot(hn_all, wxb_ref[:, k * XA_COLS:(k + 1) * XA_COLS], preferred_element_type=F32).astype(BF16)

    xb_next = xb_block(0)
    for k in range(d // XA_COLS):
        cs = slice(k * XA_COLS, (k + 1) * XA_COLS)
        xb, xb_next = xb_next, (xb_block(k + 1) if (k + 1) * XA_COLS < d else None)
        for j in range(tn):
            parts = []
            for hf in range(halves):
                c0 = (hf * tn + j) * r
                pq = jnp.dot(fs1_ref[...], xb[c0:c0 + r, :], preferred_element_type=F32)
                p, q = pq[:r], pq[r:]
                tc = tw_refs[hf][:, j:j + 1]
                ts = tw_refs[hf][:, tn + j:tn + j + 1]
                parts.append((p * tc - q * ts, p * ts + q * tc))
            if halves == 1:
                o_ref[0, :, j, cs] = _pack_complex(*parts[0])
            else:
                o_ref[0, :, j, cs] = _pack_complex(parts[0][0] + parts[1][0], parts[0][1] + parts[1][1])
                o_ref[1, :, j, cs] = _pack_complex(parts[0][0] - parts[1][0], parts[0][1] - parts[1][1])


def _fft_s1(x, w_xb, fs1, tw, *, tn, halves):
    bsz, r, ld = x.shape
    d = ld // TOK
    nc = TOK // halves
    nblk = nc // tn
    const = lambda b, i: (0, 0)
    x_specs = [pl.BlockSpec((None, r, tn * d), functools.partial(lambda b, i, hf: (b, 0, hf * nblk + i), hf=hf))
               for hf in range(halves)]
    tw_specs = [pl.BlockSpec((None, r, 2 * tn), functools.partial(lambda b, i, hf: (hf * nblk + i, 0, 0), hf=hf))
                for hf in range(halves)]
    return pl.pallas_call(
        functools.partial(_fft_s1_kernel, tn=tn, halves=halves),
        out_shape=jax.ShapeDtypeStruct((bsz, halves, r, nc, d), jnp.uint32),
        grid=(bsz, nblk),
        in_specs=x_specs + tw_specs + [
            _resident((d, d), const),
            _resident((2 * r, r), const)],
        out_specs=pl.BlockSpec((None, halves, r, tn, d), lambda b, i: (b, 0, 0, i, 0)),
        compiler_params=_cparams(2),
        name="fft_s1",
    )(*([x] * halves), *([tw] * halves), w_xb, fs1)


def _fft_s2_kernel(a_ref, x_ref, h_ref, ya_ref, mod_ref, wgm_ref, m2_ref, cc_ref, cs_ref, wbo_ref, wo_ref,
                   fg_ref, o_ref, *, tk1, final):
    r = x_ref.shape[0]
    d = fg_ref.shape[1]
    gw = d // N_FOURIER_GROUPS
    _, _, gate = _mod_rows(mod_ref, d)
    cols = [slice(q * d, (q + 1) * d) for q in range(tk1)]

    us = [jnp.dot(m2_ref[...], jnp.concatenate(_unpack_complex(a_ref[q]), axis=0),
                  preferred_element_type=F32) for q in range(tk1)]
    ur = jnp.concatenate([u[:r] for u in us], axis=0).astype(BF16)
    uq = jnp.concatenate([u[r:] for u in us], axis=0).astype(BF16)
    x = jnp.concatenate([x_ref[:, cs] for cs in cols], axis=0)
    ya = jnp.concatenate([ya_ref[:, cs] for cs in cols], axis=0)

    hn = jnp.concatenate([h_ref[:, cs] for cs in cols], axis=0).astype(BF16)
    gm = jnp.dot(hn, wgm_ref[...], preferred_element_type=F32)
    y = jnp.concatenate(
        [jnp.dot(ur[:, gi * gw:(gi + 1) * gw], cc_ref[...], preferred_element_type=F32)
         + jnp.dot(uq[:, gi * gw:(gi + 1) * gw], cs_ref[...], preferred_element_type=F32)
         for gi in range(N_FOURIER_GROUPS)], axis=1)
    yb = jnp.dot((y * _silu(gm[:, :d])).astype(BF16), wbo_ref[...], preferred_element_type=F32)
    mix = _sigmoid(gm[:, d:2 * d]) * ya + _sigmoid(gm[:, 2 * d:]) * yb
    out = jnp.dot(mix.astype(BF16), wo_ref[...], preferred_element_type=F32)
    xn = x + gate * out
    if final:
        ms = jnp.mean(xn * xn, axis=-1, keepdims=True)
        xn = xn * lax.rsqrt(ms + EPS) * fg_ref[...]
    for q, cs in enumerate(cols):
        if len(o_ref.shape) == 3:
            o_ref[:, q, :] = xn[q * r:(q + 1) * r, :]
        else:
            o_ref[:, cs] = xn[q * r:(q + 1) * r, :]


def _fft_s2(a, x, h, ya, mod, w_gm, m2, cc, cs, w_b_out, w_o, final_g, *, tk1, final):
    bsz, r, ld = x.shape
    d = ld // TOK
    halves = a.shape[1]
    nc = a.shape[3]
    nblk = (TOK // halves) // tk1
    const = lambda b, k2, i: (0, 0)
    tok = pl.BlockSpec((None, r, tk1 * d), lambda b, k2, i: (b, 0, k2 * nblk + i))
    gw = d // N_FOURIER_GROUPS
    if final and tk1 % SUBLANES == 0:
        out_shape = jax.ShapeDtypeStruct((bsz, r, TOK, d), F32)
        out_spec = pl.BlockSpec((None, r, tk1, d), lambda b, k2, i: (b, 0, k2 * nblk + i, 0))
    else:
        out_shape = jax.ShapeDtypeStruct((bsz, r, ld), F32)
        out_spec = tok
    return pl.pallas_call(
        functools.partial(_fft_s2_kernel, tk1=tk1, final=final),
        out_shape=out_shape,
        grid=(bsz, halves, nblk),
        in_specs=[pl.BlockSpec((None, None, tk1, nc, d), lambda b, k2, i: (b, k2, i, 0, 0)),
                  tok, tok, tok,
                  pl.BlockSpec((None, 1, 3 * d), lambda b, k2, i: (b, 0, 0)),
                  _resident((d, 3 * d), const),
                  pl.BlockSpec((None, 2 * nc, 2 * nc), lambda b, k2, i: (k2, 0, 0)),
                  _resident((gw, gw), const),
                  _resident((gw, gw), const),
                  _resident((d, d), const),
                  _resident((d, d), const),
                  _resident((1, d), const)],
        out_specs=out_spec,
        compiler_params=_cparams(3),
        name="fft_s2",
    )(a, x, h, ya, mod, w_gm, m2, cc, cs, w_b_out, w_o, final_g)


def _dft_tables(r, tn, gw):
    halves = TOK // r
    s = r * TOK
    k1 = np.arange(r)
    ang1 = 2.0 * np.pi * np.outer(k1, k1) / r
    fs1 = np.concatenate([np.cos(ang1), np.sin(ang1)], axis=0)
    tok = np.arange(TOK)
    ang_t = 2.0 * np.pi * np.outer(k1, tok) / s
    scale = 1.0 / np.sqrt(float(s) * gw)
    tc = (np.cos(ang_t) * scale).reshape(r, TOK // tn, tn).transpose(1, 0, 2)
    ts = (np.sin(ang_t) * scale).reshape(r, TOK // tn, tn).transpose(1, 0, 2)
    tw = np.concatenate([tc, ts], axis=2)
    n = np.arange(r)
    m2 = []
    for k2 in range(halves):
        ang2 = 2.0 * np.pi * np.outer(halves * n + k2, n) / (halves * r)
        c2, s2 = np.cos(ang2), np.sin(ang2)
        m2.append(np.block([[c2, -s2], [s2, c2]]))
    c = np.arange(gw)
    angc = 2.0 * np.pi * np.outer(c, c) / gw
    return (jnp.asarray(fs1, BF16), jnp.asarray(tw, F32), jnp.asarray(np.stack(m2), BF16),
            jnp.asarray(np.cos(angc), BF16), jnp.asarray(-np.sin(angc), BF16))


def _trunk(x, mods, lw, final_g):
    bsz, s, d = x.shape
    r = s // TOK
    halves = TOK // r
    assert r * TOK == s and r * halves == TOK and halves in (1, 2) and r % SUBLANES == 0
    tn = 8
    tk1 = 512 // r
    gw = d // N_FOURIER_GROUPS
    fs1, tw, m2, cc, cs = _dft_tables(r, tn, gw)
    depth = len(lw)
    for l in range(depth):
        w = lw[l]
        mod = mods[l]
        res = _lru_fwd(x, mod, w["norm_g"], w["w_xa"], w["conv_w"], w["conv_b"], w["wg"][0], w["lam"][0],
                       natural_in=(l == 0))
        xc, hf, hbf, h = res[:4]
        if l == 0:
            x = res[4]
        ya = _lru_bwd(xc, hf, hbf, w["w_ga"], w["wg"][1], w["lam"][1], w["w_a_out"])
        a = _fft_s1(h, w["w_xb"], fs1, tw, tn=tn, halves=halves)
        x = _fft_s2(a, x, h, ya, mod, w["w_gm"], m2, cc, cs, w["w_b_out"], w["w_o"], final_g,
                    tk1=tk1, final=(l == depth - 1))
    return x.reshape(bsz, s, d)


def _gate_weights(w, b):
    n_heads, blk = w.shape[1], w.shape[2]
    bias = 0.5 * jnp.concatenate([b[0].reshape(n_heads, 1, blk), b[1].reshape(n_heads, 1, blk)], axis=-1)
    rows = jnp.concatenate([jnp.concatenate([w[0], w[1]], axis=-1), bias,
                            jnp.zeros((n_heads, blk - 1, 2 * blk), F32)], axis=1)
    return rows.astype(BF16)


def _layer_weights(norm_g, w_in, conv_w, conv_b, w_rg, b_rg, lam, w_a_out, w_b_out, w_o):
    depth, d = norm_g.shape
    lw = []
    for l in range(depth):
        wi = w_in[l].astype(BF16)
        lw.append(dict(
            norm_g=norm_g[l].reshape(1, d),
            w_xa=wi[:, 0:d], w_ga=wi[:, d:2 * d], w_xb=wi[:, 2 * d:3 * d], w_gm=wi[:, 3 * d:6 * d],
            conv_w=conv_w[l], conv_b=conv_b[l].reshape(1, d),
            wg=[_gate_weights(w_rg[l, dr], b_rg[l, dr]) for dr in range(2)],
            lam=[lam[l, dr].reshape(1, d) for dr in range(2)],
            w_a_out=w_a_out[l].astype(BF16), w_b_out=w_b_out[l].astype(BF16), w_o=w_o[l].astype(BF16)))
    return lw


def kernel(x_prompt, x_sample, c_prompt, c_sample, norm_g, w_ada, b_ada, w_in, conv_w, conv_b, w_rg, b_rg, lam,
           w_a_out, w_b_out, w_o, final_g):
    d = x_prompt.shape[-1]
    depth = norm_g.shape[0]
    bp, bs = c_prompt.shape[0], c_sample.shape[0]
    rows = -(-(bp + bs) // SUBLANES) * SUBLANES
    c_all = jnp.concatenate([c_prompt, c_sample, jnp.zeros((rows - bp - bs, d), F32)], axis=0)
    mod = _adaln_mod(c_all, w_ada, b_ada)
    mods_p = [mod[l, 0:bp].reshape(bp, 1, 3 * d) for l in range(depth)]
    mods_s = [mod[l, bp:bp + bs].reshape(bs, 1, 3 * d) for l in range(depth)]
    lw = _layer_weights(norm_g, w_in, conv_w, conv_b, w_rg, b_rg, lam, w_a_out, w_b_out, w_o)
    fg = final_g.reshape(1, d)
    return (_trunk(x_prompt, mods_p, lw, fg), _trunk(x_sample, mods_s, lw, fg))
```

```python
import functools

import numpy as np
import jax
import jax.numpy as jnp
from jax import lax
from jax.experimental import pallas as pl
from jax.experimental.pallas import tpu as pltpu

F32 = jnp.float32
BF16 = jnp.bfloat16

N_LRU_HEADS = 8
N_FOURIER_GROUPS = 4
CONV_WIDTH = 4
CONV_LEFT = 2
LRU_C = 8.0
EPS = 1e-6

SUBLANES = 8
TOK = 128
ROW_CHUNK = 512
XA_COLS = 256
VMEM_LIMIT = 58 * 1024 * 1024
TINY = 1e-30
LOG2E = 1.4426950408889634


def _cparams(n_axes):
    return pltpu.CompilerParams(dimension_semantics=("arbitrary",) * n_axes,
                                vmem_limit_bytes=VMEM_LIMIT)


def _resident(shape, index_map):
    return pl.BlockSpec(shape, index_map, pipeline_mode=pl.Buffered(1))


def _sigmoid(z):
    return 0.5 * jnp.tanh(0.5 * z) + 0.5


def _silu(z):
    zh = 0.5 * z
    return zh * (jnp.tanh(zh) + 1.0)


def _softplus(z):
    return jnp.maximum(z, 0.0) + jnp.log1p(jnp.exp(-jnp.abs(z)))


def _norm_mod(x, g1s, shift):
    ms = jnp.mean(x * x, axis=-1, keepdims=True)
    return x * lax.rsqrt(ms + EPS) * g1s + shift


def _pack_pair(hi, lo):
    h = lax.bitcast_convert_type(hi.astype(BF16).astype(F32), jnp.uint32)
    l = lax.bitcast_convert_type(lo.astype(BF16).astype(F32), jnp.uint32)
    return h | (l >> 16)


def _unpack_pair(w):
    return (lax.bitcast_convert_type(w & jnp.uint32(0xFFFF0000), F32),
            lax.bitcast_convert_type(w << 16, F32))


def _mod_rows(mod_ref, d):
    return mod_ref[:, 0:d], mod_ref[:, d:2 * d], mod_ref[:, 2 * d:3 * d]


def _norm_to_rows(x_ref, hbf_ref, g1s, shift, n_groups, xs_ref=None):
    d = g1s.shape[1]
    g = SUBLANES

    def group(l):
        if xs_ref is None:
            return x_ref[:, l * d:(l + 1) * d]
        xg = x_ref[:, l, :]
        xs_ref[:, l * d:(l + 1) * d] = xg
        return xg

    for l in range(0, n_groups, 2):
        xx = jnp.concatenate([group(l), group(l + 1)], axis=0)
        hbf_ref[l * g:(l + 2) * g, :] = _norm_mod(xx, g1s, shift).astype(BF16)


def _mod_kernel(c_ref, w_ref, b_ref, o_ref):
    c = c_ref[...]
    o_ref[...] = jnp.dot(_silu(c), w_ref[...], preferred_element_type=F32,
                         precision=lax.Precision.HIGHEST) + b_ref[...]


def _adaln_mod(c_all, w_ada, b_ada):
    depth, d, d3 = w_ada.shape
    rows = c_all.shape[0]
    return pl.pallas_call(
        _mod_kernel,
        out_shape=jax.ShapeDtypeStruct((depth, rows, d3), F32),
        grid=(depth, d3 // d),
        in_specs=[pl.BlockSpec((rows, d), lambda l, j: (0, 0)),
                  pl.BlockSpec((None, d, d), lambda l, j: (l, 0, j)),
                  pl.BlockSpec((None, 1, d), lambda l, j: (l, 0, j))],
        out_specs=pl.BlockSpec((None, rows, d), lambda l, j: (l, 0, j)),
        compiler_params=_cparams(2),
        name="adaln_mod",
    )(c_all, w_ada, b_ada.reshape(depth, 1, d3))


def _lru_coeffs(xh, gh, k2):
    hb = xh.shape[1]
    t_r = jnp.tanh(gh[:, :hb])
    t_i = jnp.tanh(gh[:, hb:])
    a = jnp.exp2((t_r + 1.0) * k2)
    y = 1.0 - a * a
    mult = y * lax.rsqrt(jnp.maximum(y, TINY))
    u = (t_i + 1.0) * xh
    return a, mult * u, u


def _decay_const(lam_ref):
    return (-0.5 * LRU_C * LOG2E) * _softplus(-lam_ref[...])


def _gate_dot(xhb, ones, wg):
    return jnp.dot(jnp.concatenate([xhb, ones], axis=1), wg, preferred_element_type=F32)


def _ones_column(n_rows, width):
    lane = lax.broadcasted_iota(jnp.int32, (n_rows, width), 1)
    return jnp.where(lane == 0, 1.0, 0.0).astype(BF16)


def _segment_ends(a_ref, b_ref, reverse):
    g = SUBLANES
    n_groups = a_ref.shape[0] // g
    d = a_ref.shape[1]

    def body(s, hp):
        l = (n_groups - 1 - s) if reverse else s
        r0 = pl.multiple_of(l * g, g)
        a = a_ref[pl.ds(r0, g), :]
        return (a * hp[0] + b_ref[pl.ds(r0, g), :], a * hp[1])

    return lax.fori_loop(0, n_groups, body, (jnp.zeros((g, d), F32), jnp.ones((g, d), F32)), unroll=8)


def _segment_apply(a_ref, b_ref, c_in, out_ref, reverse):
    g = SUBLANES
    n_pairs = a_ref.shape[0] // (2 * g)

    def body(s, h):
        p = (n_pairs - 1 - s) if reverse else s
        r0 = pl.multiple_of(p * 2 * g, 2 * g)
        a = a_ref[pl.ds(r0, 2 * g), :]
        b = b_ref[pl.ds(r0, 2 * g), :]
        if reverse:
            h1 = a[g:] * h + b[g:]
            h0 = a[:g] * h1 + b[:g]
            h = h0
        else:
            h0 = a[:g] * h + b[:g]
            h1 = a[g:] * h0 + b[g:]
            h = h1
        out_ref[pl.ds(r0, 2 * g), :] = jnp.concatenate([h0, h1], axis=0).astype(out_ref.dtype)
        return h

    lax.fori_loop(0, n_pairs, body, c_in, unroll=4)


def _segment_carries(h_end, p_end, carry, reverse):
    rows = [None] * SUBLANES
    c = carry
    for s in (range(SUBLANES - 1, -1, -1) if reverse else range(SUBLANES)):
        rows[s] = c
        c = h_end[s:s + 1, :] + p_end[s:s + 1, :] * c
    return jnp.concatenate(rows, axis=0), c


def _lru_fwd_kernel(*refs, n_tiles, natural_in):
    (x_ref, xn_ref, mod_ref, g_ref, wxa_ref, cw_ref, cb_ref, wg_ref, lam_ref, wgb_ref,
     xc_ref, hf_ref, hbf_ref, hs_ref) = refs[:14]
    xs_ref = refs[14] if natural_in else None
    ext_ref, a_ref, b_ref, carry_ref, prev_ref, u_ref = refs[-6:]
    i = pl.program_id(1)
    g = SUBLANES
    d = g_ref.shape[1]
    n_rows = a_ref.shape[0]
    n_groups = n_rows // g
    hb = d // N_LRU_HEADS
    shift, scale, _ = _mod_rows(mod_ref, d)
    g1s = g_ref[...] * (1.0 + scale)

    @pl.when(i == 0)
    def _():
        carry_ref[...] = jnp.zeros_like(carry_ref)
        prev_ref[...] = jnp.zeros_like(prev_ref)

    _norm_to_rows(x_ref, hbf_ref, g1s, shift, n_groups, xs_ref)
    xn = xn_ref[:, 0, :] if natural_in else xn_ref[...]
    hbf_ref[n_rows:n_rows + 2 * g, :] = _norm_mod(jnp.concatenate([xn, xn], axis=0), g1s, shift).astype(BF16)
    row = lax.broadcasted_iota(jnp.int32, (g, XA_COLS), 0)
    nxt_scale = jnp.where(i == n_tiles - 1, 0.0, 1.0)

    def xa_block(k):
        cs = slice(k * XA_COLS, (k + 1) * XA_COLS)
        ext_ref[CONV_LEFT * g:, cs] = jnp.dot(hbf_ref[...], wxa_ref[:, cs], preferred_element_type=F32)
        last1 = ext_ref[(n_groups + 1) * g:(n_groups + 2) * g, cs]
        last2 = ext_ref[n_groups * g:(n_groups + 1) * g, cs]
        first = ext_ref[CONV_LEFT * g:(CONV_LEFT + 1) * g, cs]
        nxt = ext_ref[(n_groups + 2) * g:(n_groups + 3) * g, cs]
        ext_ref[g:2 * g, cs] = jnp.where(row == 0, pltpu.roll(prev_ref[g:2 * g, cs], 1, axis=0),
                                         pltpu.roll(last1, 1, axis=0))
        ext_ref[0:g, cs] = jnp.where(row == 0, pltpu.roll(prev_ref[0:g, cs], 1, axis=0),
                                     pltpu.roll(last2, 1, axis=0))
        prev_ref[0:g, cs] = last2
        prev_ref[g:2 * g, cs] = last1
        ext_ref[(n_groups + 2) * g:(n_groups + 3) * g, cs] = jnp.where(
            row == g - 1, pltpu.roll(nxt, g - 1, axis=0) * nxt_scale, pltpu.roll(first, g - 1, axis=0))

    k2 = _decay_const(lam_ref)
    ones = _ones_column(n_rows, hb)
    hcw = 0.5 * cw_ref[...]
    hcb = 0.5 * cb_ref[...]
    heads_per_block = XA_COLS // hb
    xa_block(0)
    for h in range(N_LRU_HEADS):
        if h % heads_per_block == 0:
            if (h // heads_per_block + 1) * XA_COLS < d:
                xa_block(h // heads_per_block + 1)
            c0 = h * hb
            sgb = _silu(jnp.dot(hbf_ref[0:n_rows, :], wgb_ref[:, c0:c0 + XA_COLS], preferred_element_type=F32))
            w = _pack_pair(hbf_ref[0:n_rows, c0:c0 + XA_COLS].astype(F32), sgb)
            for l in range(n_groups):
                hs_ref[:, l * d + c0:l * d + c0 + XA_COLS] = w[l * g:(l + 1) * g, :]
        sl = slice(h * hb, (h + 1) * hb)
        xh = ext_ref[0:n_rows, sl] * hcw[0:1, sl] + hcb[:, sl]
        for kk in range(1, CONV_WIDTH):
            xh = xh + ext_ref[kk * g:kk * g + n_rows, sl] * hcw[kk:kk + 1, sl]
        xhb = xh.astype(BF16)
        xc_ref[:, sl] = xhb
        a, b, u = _lru_coeffs(xh, _gate_dot(xhb, ones, wg_ref[h]), k2[:, sl])
        a_ref[:, sl] = a
        b_ref[:, sl] = b
        u_ref[:, sl] = u[0:g, :]

    @pl.when(i == 0)
    def _():
        first_row = lax.broadcasted_iota(jnp.int32, (g, d), 0) == 0
        b_ref[0:g, :] = jnp.where(first_row, u_ref[...], b_ref[0:g, :])

    h_end, p_end = _segment_ends(a_ref, b_ref, False)
    c_in, carry_ref[...] = _segment_carries(h_end, p_end, carry_ref[...], False)
    _segment_apply(a_ref, b_ref, c_in, hf_ref, False)


def _lru_fwd(x, mod, norm_g, w_xa, conv_w, conv_b, wg, lam, w_gb, *, natural_in):
    g = SUBLANES
    if natural_in:
        bsz, s, d = x.shape
        r = s // TOK
        x = x.reshape(bsz, r, TOK, d)
        x_spec = pl.BlockSpec((None, g, TOK, d), lambda b, i: (b, i, 0, 0))
        xn_spec = pl.BlockSpec((None, g, g, d), lambda b, i: (b, jnp.minimum(i + 1, n_tiles - 1), 0, 0))
    else:
        bsz, r, ld = x.shape
        d = ld // TOK
        x_spec = pl.BlockSpec((None, g, TOK * d), lambda b, i: (b, i, 0))
        xn_spec = pl.BlockSpec((None, g, d), lambda b, i: (b, jnp.minimum(i + 1, n_tiles - 1), 0))
    n_tiles = r // g
    n_rows = TOK * g
    hb = d // N_LRU_HEADS
    const = lambda b, i: (0, 0)
    rowform = jax.ShapeDtypeStruct((bsz, n_tiles, n_rows, d), BF16)
    rowspec = pl.BlockSpec((None, None, n_rows, d), lambda b, i: (b, i, 0, 0))
    storage = jax.ShapeDtypeStruct((bsz, r, TOK * d), F32)
    storage_spec = pl.BlockSpec((None, g, TOK * d), lambda b, i: (b, i, 0))
    out_shape = [rowform, rowform, jax.ShapeDtypeStruct((bsz, n_tiles, n_rows + 2 * g, d), BF16),
                 jax.ShapeDtypeStruct((bsz, r, TOK * d), jnp.uint32)]
    out_specs = [rowspec, rowspec, pl.BlockSpec((None, None, n_rows + 2 * g, d), lambda b, i: (b, i, 0, 0)),
                 storage_spec]
    if natural_in:
        out_shape.append(storage)
        out_specs.append(storage_spec)
    return pl.pallas_call(
        functools.partial(_lru_fwd_kernel, n_tiles=n_tiles, natural_in=natural_in),
        out_shape=tuple(out_shape),
        grid=(bsz, n_tiles),
        in_specs=[x_spec, xn_spec,
                  pl.BlockSpec((None, 1, 3 * d), lambda b, i: (b, 0, 0)),
                  _resident((1, d), const),
                  _resident((d, d), const),
                  _resident((CONV_WIDTH, d), const),
                  _resident((1, d), const),
                  _resident((N_LRU_HEADS, 2 * hb, 2 * hb), lambda b, i: (0, 0, 0)),
                  _resident((1, d), const),
                  _resident((d, d), const)],
        out_specs=tuple(out_specs),
        scratch_shapes=[pltpu.VMEM((n_rows + (CONV_LEFT + 2) * g, d), F32),
                        pltpu.VMEM((n_rows, d), F32),
                        pltpu.VMEM((n_rows, d), F32),
                        pltpu.VMEM((1, d), F32),
                        pltpu.VMEM((CONV_LEFT * g, d), F32),
                        pltpu.VMEM((g, d), F32)],
        compiler_params=_cparams(2),
        name="lru_fwd",
    )(x, x, mod, norm_g, w_xa, conv_w, conv_b, wg, lam, w_gb)


def _lru_bwd_kernel(xc_ref, hf_ref, hbf_ref, wga_ref, wg_ref, lam_ref, wao_ref,
                    ya_ref, a_ref, b_ref, sg_ref, carry_ref, u_ref):
    i = pl.program_id(1)
    g = SUBLANES
    n_rows, d = a_ref.shape
    hb = d // N_LRU_HEADS

    @pl.when(i == 0)
    def _():
        carry_ref[...] = jnp.zeros_like(carry_ref)

    k2 = _decay_const(lam_ref)
    ones = _ones_column(n_rows, hb)
    heads_per_block = XA_COLS // hb
    for h in range(N_LRU_HEADS):
        if h % heads_per_block == 0:
            cs = slice(h * hb, h * hb + XA_COLS)
            sg_ref[:, cs] = _silu(jnp.dot(hbf_ref[0:n_rows, :], wga_ref[:, cs], preferred_element_type=F32))
        sl = slice(h * hb, (h + 1) * hb)
        xhb = xc_ref[:, sl]
        a, b, u = _lru_coeffs(xhb.astype(F32), _gate_dot(xhb, ones, wg_ref[h]), k2[:, sl])
        a_ref[:, sl] = a
        b_ref[:, sl] = b
        u_ref[:, sl] = u[n_rows - g:, :]

    @pl.when(i == 0)
    def _():
        row = lax.broadcasted_iota(jnp.int32, (g, d), 0)
        b_ref[n_rows - g:, :] = jnp.where(row == g - 1, u_ref[...], b_ref[n_rows - g:, :])

    h_end, p_end = _segment_ends(a_ref, b_ref, True)
    c_in, carry_ref[...] = _segment_carries(h_end, p_end, carry_ref[...], True)
    _segment_apply(a_ref, b_ref, c_in, b_ref, True)

    for r0 in range(0, n_rows, ROW_CHUNK):
        rs = slice(r0, r0 + ROW_CHUNK)
        ya = (hf_ref[rs, :].astype(F32) + b_ref[rs, :]) * sg_ref[rs, :]
        out = jnp.dot(ya.astype(BF16), wao_ref[...], preferred_element_type=F32)
        for gl in range(ROW_CHUNK // g):
            l = r0 // g + gl
            ya_ref[:, l * d:(l + 1) * d] = out[gl * g:(gl + 1) * g, :]


def _lru_bwd(xc, hf, hbf, w_ga, wg, lam, w_a_out):
    bsz, n_tiles, n_rows, d = xc.shape
    g = SUBLANES
    hb = d // N_LRU_HEADS
    const = lambda b, i: (0, 0)
    rev3 = lambda b, i: (b, n_tiles - 1 - i, 0)
    rev4 = lambda b, i: (b, n_tiles - 1 - i, 0, 0)
    return pl.pallas_call(
        _lru_bwd_kernel,
        out_shape=jax.ShapeDtypeStruct((bsz, n_tiles * g, TOK * d), F32),
        grid=(bsz, n_tiles),
        in_specs=[pl.BlockSpec((None, None, n_rows, d), rev4),
                  pl.BlockSpec((None, None, n_rows, d), rev4),
                  pl.BlockSpec((None, None, n_rows + 2 * g, d), rev4),
                  _resident((d, d), const),
                  _resident((N_LRU_HEADS, 2 * hb, 2 * hb), lambda b, i: (0, 0, 0)),
                  _resident((1, d), const),
                  _resident((d, d), const)],
        out_specs=pl.BlockSpec((None, g, TOK * d), rev3),
        scratch_shapes=[pltpu.VMEM((n_rows, d), F32),
                        pltpu.VMEM((n_rows, d), F32),
                        pltpu.VMEM((n_rows, d), F32),
                        pltpu.VMEM((1, d), F32),
                        pltpu.VMEM((g, d), F32)],
        compiler_params=_cparams(2),
        name="lru_bwd",
    )(xc, hf, hbf, w_ga, wg, lam, w_a_out)


def _fft_s1_kernel(*refs, tn, halves):
    x_refs = refs[:halves]
    tw_refs = refs[halves:2 * halves]
    wxb_ref, fs1_ref, o_ref = refs[2 * halves:]
    r = x_refs[0].shape[0]
    d = wxb_ref.shape[0]
    hn_all = jnp.concatenate(
        [_unpack_pair(x_refs[hf][:, j * d:(j + 1) * d])[0].astype(BF16)
         for hf in range(halves) for j in range(tn)], axis=0)
    xb_all = jnp.dot(hn_all, wxb_ref[...], preferred_element_type=F32).astype(BF16)
    for j in range(tn):
        parts = []
        for hf in range(halves):
            c0 = (hf * tn + j) * r
            pq = jnp.dot(fs1_ref[...], xb_all[c0:c0 + r, :], preferred_element_type=F32)
            p, q = pq[:r], pq[r:]
            tc = tw_refs[hf][:, j:j + 1]
            ts = tw_refs[hf][:, tn + j:tn + j + 1]
            parts.append((p * tc - q * ts, p * ts + q * tc))
        if halves == 1:
            o_ref[0, :, j, :] = _pack_pair(*parts[0])
        else:
            o_ref[0, :, j, :] = _pack_pair(parts[0][0] + parts[1][0], parts[0][1] + parts[1][1])
            o_ref[1, :, j, :] = _pack_pair(parts[0][0] - parts[1][0], parts[0][1] - parts[1][1])


def _fft_s1(x, w_xb, fs1, tw, *, tn, halves):
    bsz, r, ld = x.shape
    d = ld // TOK
    nc = TOK // halves
    nblk = nc // tn
    const = lambda b, i: (0, 0)
    x_specs = [pl.BlockSpec((None, r, tn * d), functools.partial(lambda b, i, hf: (b, 0, hf * nblk + i), hf=hf))
               for hf in range(halves)]
    tw_specs = [pl.BlockSpec((None, r, 2 * tn), functools.partial(lambda b, i, hf: (hf * nblk + i, 0, 0), hf=hf))
                for hf in range(halves)]
    return pl.pallas_call(
        functools.partial(_fft_s1_kernel, tn=tn, halves=halves),
        out_shape=jax.ShapeDtypeStruct((bsz, halves, r, nc, d), jnp.uint32),
        grid=(bsz, nblk),
        in_specs=x_specs + tw_specs + [
            _resident((d, d), const),
            _resident((2 * r, r), const)],
        out_specs=pl.BlockSpec((None, halves, r, tn, d), lambda b, i: (b, 0, 0, i, 0)),
        compiler_params=_cparams(2),
        name="fft_s1",
    )(*([x] * halves), *([tw] * halves), w_xb, fs1)


def _fft_s2_kernel(a_ref, x_ref, h_ref, ya_ref, mod_ref, wm_ref, m2_ref, cc_ref, cs_ref, wbo_ref, wo_ref,
                   fg_ref, o_ref, *, tk1, final):
    r = x_ref.shape[0]
    d = fg_ref.shape[1]
    gw = d // N_FOURIER_GROUPS
    _, _, gate = _mod_rows(mod_ref, d)
    cols = [slice(q * d, (q + 1) * d) for q in range(tk1)]

    us = []
    for q in range(tk1):
        re, im = _unpack_pair(a_ref[q])
        rhs = jnp.concatenate([re.astype(BF16), im.astype(BF16)], axis=0)
        us.append(jnp.dot(m2_ref[...], rhs, preferred_element_type=F32))
    ur = jnp.concatenate([u[:r] for u in us], axis=0).astype(BF16)
    uq = jnp.concatenate([u[r:] for u in us], axis=0).astype(BF16)
    x = jnp.concatenate([x_ref[:, cs] for cs in cols], axis=0)
    ya = jnp.concatenate([ya_ref[:, cs] for cs in cols], axis=0)

    hn, sgb = _unpack_pair(jnp.concatenate([h_ref[:, cs] for cs in cols], axis=0))
    gm = jnp.dot(hn.astype(BF16), wm_ref[...], preferred_element_type=F32)
    y = jnp.concatenate(
        [jnp.dot(ur[:, gi * gw:(gi + 1) * gw], cc_ref[...], preferred_element_type=F32)
         + jnp.dot(uq[:, gi * gw:(gi + 1) * gw], cs_ref[...], preferred_element_type=F32)
         for gi in range(N_FOURIER_GROUPS)], axis=1)
    yb = jnp.dot((y * sgb).astype(BF16), wbo_ref[...], preferred_element_type=F32)
    mix = _sigmoid(gm[:, :d]) * ya + _sigmoid(gm[:, d:]) * yb
    out = jnp.dot(mix.astype(BF16), wo_ref[...], preferred_element_type=F32)
    xn = x + gate * out
    if final:
        ms = jnp.mean(xn * xn, axis=-1, keepdims=True)
        xn = xn * lax.rsqrt(ms + EPS) * fg_ref[...]
    for q, cs in enumerate(cols):
        if len(o_ref.shape) == 3:
            o_ref[:, q, :] = xn[q * r:(q + 1) * r, :]
        else:
            o_ref[:, cs] = xn[q * r:(q + 1) * r, :]


def _fft_s2(a, x, h, ya, mod, w_m, m2, cc, cs, w_b_out, w_o, final_g, *, tk1, final):
    bsz, r, ld = x.shape
    d = ld // TOK
    halves = a.shape[1]
    nc = a.shape[3]
    nblk = (TOK // halves) // tk1
    const = lambda b, k2, i: (0, 0)
    tok = pl.BlockSpec((None, r, tk1 * d), lambda b, k2, i: (b, 0, k2 * nblk + i))
    gw = d // N_FOURIER_GROUPS
    if final and tk1 % SUBLANES == 0:
        out_shape = jax.ShapeDtypeStruct((bsz, r, TOK, d), F32)
        out_spec = pl.BlockSpec((None, r, tk1, d), lambda b, k2, i: (b, 0, k2 * nblk + i, 0))
    else:
        out_shape = jax.ShapeDtypeStruct((bsz, r, ld), F32)
        out_spec = tok
    return pl.pallas_call(
        functools.partial(_fft_s2_kernel, tk1=tk1, final=final),
        out_shape=out_shape,
        grid=(bsz, halves, nblk),
        in_specs=[pl.BlockSpec((None, None, tk1, nc, d), lambda b, k2, i: (b, k2, i, 0, 0)),
                  tok, tok, tok,
                  pl.BlockSpec((None, 1, 3 * d), lambda b, k2, i: (b, 0, 0)),
                  _resident((d, 2 * d), const),
                  pl.BlockSpec((None, 2 * nc, 2 * nc), lambda b, k2, i: (k2, 0, 0)),
                  _resident((gw, gw), const),
                  _resident((gw, gw), const),
                  _resident((d, d), const),
                  _resident((d, d), const),
                  _resident((1, d), const)],
        out_specs=out_spec,
        compiler_params=_cparams(3),
        name="fft_s2",
    )(a, x, h, ya, mod, w_m, m2, cc, cs, w_b_out, w_o, final_g)


def _dft_tables(r, tn, gw):
    halves = TOK // r
    s = r * TOK
    k1 = np.arange(r)
    ang1 = 2.0 * np.pi * np.outer(k1, k1) / r
    fs1 = np.concatenate([np.cos(ang1), np.sin(ang1)], axis=0)
    tok = np.arange(TOK)
    ang_t = 2.0 * np.pi * np.outer(k1, tok) / s
    scale = 1.0 / np.sqrt(float(s) * gw)
    tc = (np.cos(ang_t) * scale).reshape(r, TOK // tn, tn).transpose(1, 0, 2)
    ts = (np.sin(ang_t) * scale).reshape(r, TOK // tn, tn).transpose(1, 0, 2)
    tw = np.concatenate([tc, ts], axis=2)
    n = np.arange(r)
    m2 = []
    for k2 in range(halves):
        ang2 = 2.0 * np.pi * np.outer(halves * n + k2, n) / (halves * r)
        c2, s2 = np.cos(ang2), np.sin(ang2)
        m2.append(np.block([[c2, -s2], [s2, c2]]))
    c = np.arange(gw)
    angc = 2.0 * np.pi * np.outer(c, c) / gw
    return (jnp.asarray(fs1, BF16), jnp.asarray(tw, F32), jnp.asarray(np.stack(m2), BF16),
            jnp.asarray(np.cos(angc), BF16), jnp.asarray(-np.sin(angc), BF16))


def _trunk(x, mods, lw, final_g):
    bsz, s, d = x.shape
    r = s // TOK
    halves = TOK // r
    assert r * TOK == s and r * halves == TOK and halves in (1, 2) and r % SUBLANES == 0
    tn = 8
    tk1 = 512 // r
    gw = d // N_FOURIER_GROUPS
    fs1, tw, m2, cc, cs = _dft_tables(r, tn, gw)
    depth = len(lw)
    for l in range(depth):
        w = lw[l]
        mod = mods[l]
        res = _lru_fwd(x, mod, w["norm_g"], w["w_xa"], w["conv_w"], w["conv_b"], w["wg"][0], w["lam"][0],
                       w["w_gb"], natural_in=(l == 0))
        xc, hf, hbf, h = res[:4]
        if l == 0:
            x = res[4]
        ya = _lru_bwd(xc, hf, hbf, w["w_ga"], w["wg"][1], w["lam"][1], w["w_a_out"])
        a = _fft_s1(h, w["w_xb"], fs1, tw, tn=tn, halves=halves)
        x = _fft_s2(a, x, h, ya, mod, w["w_m"], m2, cc, cs, w["w_b_out"], w["w_o"], final_g,
                    tk1=tk1, final=(l == depth - 1))
    return x.reshape(bsz, s, d)


def _gate_weights(w, b):
    n_heads, blk = w.shape[1], w.shape[2]
    bias = 0.5 * jnp.concatenate([b[0].reshape(n_heads, 1, blk), b[1].reshape(n_heads, 1, blk)], axis=-1)
    rows = jnp.concatenate([jnp.concatenate([w[0], w[1]], axis=-1), bias,
                            jnp.zeros((n_heads, blk - 1, 2 * blk), F32)], axis=1)
    return rows.astype(BF16)


def _layer_weights(norm_g, w_in, conv_w, conv_b, w_rg, b_rg, lam, w_a_out, w_b_out, w_o):
    depth, d = norm_g.shape
    lw = []
    for l in range(depth):
        wi = w_in[l].astype(BF16)
        lw.append(dict(
            norm_g=norm_g[l].reshape(1, d),
            w_xa=wi[:, 0:d], w_ga=wi[:, d:2 * d], w_xb=wi[:, 2 * d:3 * d], w_gb=wi[:, 3 * d:4 * d],
            w_m=wi[:, 4 * d:6 * d],
            conv_w=conv_w[l], conv_b=conv_b[l].reshape(1, d),
            wg=[_gate_weights(w_rg[l, dr], b_rg[l, dr]) for dr in range(2)],
            lam=[lam[l, dr].reshape(1, d) for dr in range(2)],
            w_a_out=w_a_out[l].astype(BF16), w_b_out=w_b_out[l].astype(BF16), w_o=w_o[l].astype(BF16)))
    return lw


def kernel(x_prompt, x_sample, c_prompt, c_sample, norm_g, w_ada, b_ada, w_in, conv_w, conv_b, w_rg, b_rg, lam,
           w_a_out, w_b_out, w_o, final_g):
    d = x_prompt.shape[-1]
    depth = norm_g.shape[0]
    bp, bs = c_prompt.shape[0], c_sample.shape[0]
    rows = -(-(bp + bs) // SUBLANES) * SUBLANES
    c_all = jnp.concatenate([c_prompt, c_sample, jnp.zeros((rows - bp - bs, d), F32)], axis=0)
    mod = _adaln_mod(c_all, w_ada, b_ada)
    mods_p = [mod[l, 0:bp].reshape(bp, 1, 3 * d) for l in range(depth)]
    mods_s = [mod[l, bp:bp + bs].reshape(bs, 1, 3 * d) for l in range(depth)]
    lw = _layer_weights(norm_g, w_in, conv_w, conv_b, w_rg, b_rg, lam, w_a_out, w_b_out, w_o)
    fg = final_g.reshape(1, d)
    return (_trunk(x_prompt, mods_p, lw, fg), _trunk(x_sample, mods_s, lw, fg))
```

```python
import functools

import numpy as np
import jax
import jax.numpy as jnp
from jax import lax
from jax.experimental import pallas as pl
from jax.experimental.pallas import tpu as pltpu

F32 = jnp.float32
BF16 = jnp.bfloat16

N_LRU_HEADS = 8
N_FOURIER_GROUPS = 4
CONV_WIDTH = 4
CONV_LEFT = 2
LRU_C = 8.0
EPS = 1e-6

SUBLANES = 8
TOK = 128
ROW_CHUNK = 512
XA_COLS = 256
VMEM_LIMIT = 58 * 1024 * 1024
TINY = 1e-30
LOG2E = 1.4426950408889634


def _cparams(n_axes):
    return pltpu.CompilerParams(dimension_semantics=("arbitrary",) * n_axes,
                                vmem_limit_bytes=VMEM_LIMIT)


def _resident(shape, index_map):
    return pl.BlockSpec(shape, index_map, pipeline_mode=pl.Buffered(1))


def _sigmoid(z):
    return 0.5 * jnp.tanh(0.5 * z) + 0.5


def _silu(z):
    zh = 0.5 * z
    return zh * (jnp.tanh(zh) + 1.0)


def _softplus(z):
    return jnp.maximum(z, 0.0) + jnp.log1p(jnp.exp(-jnp.abs(z)))


def _norm_mod(x, g1s, shift):
    ms = jnp.mean(x * x, axis=-1, keepdims=True)
    return x * lax.rsqrt(ms + EPS) * g1s + shift


def _pack_pair(hi, lo):
    h = lax.bitcast_convert_type(hi.astype(BF16).astype(F32), jnp.uint32)
    l = lax.bitcast_convert_type(lo.astype(BF16).astype(F32), jnp.uint32)
    return h | (l >> 16)


def _unpack_pair(w):
    return (lax.bitcast_convert_type(w & jnp.uint32(0xFFFF0000), F32),
            lax.bitcast_convert_type(w << 16, F32))


def _mod_rows(mod_ref, d):
    return mod_ref[:, 0:d], mod_ref[:, d:2 * d], mod_ref[:, 2 * d:3 * d]


def _norm_to_rows(x_ref, hbf_ref, g1s, shift, n_groups, xs_ref=None):
    d = g1s.shape[1]
    g = SUBLANES

    def group(l):
        if xs_ref is None:
            return x_ref[:, l * d:(l + 1) * d]
        xg = x_ref[:, l, :]
        xs_ref[:, l * d:(l + 1) * d] = xg
        return xg

    for l in range(0, n_groups, 2):
        xx = jnp.concatenate([group(l), group(l + 1)], axis=0)
        hbf_ref[l * g:(l + 2) * g, :] = _norm_mod(xx, g1s, shift).astype(BF16)


def _mod_kernel(c_ref, w_ref, b_ref, o_ref):
    c = c_ref[...]
    o_ref[...] = jnp.dot(_silu(c), w_ref[...], preferred_element_type=F32,
                         precision=lax.Precision.HIGHEST) + b_ref[...]


def _adaln_mod(c_all, w_ada, b_ada):
    depth, d, d3 = w_ada.shape
    rows = c_all.shape[0]
    return pl.pallas_call(
        _mod_kernel,
        out_shape=jax.ShapeDtypeStruct((depth, rows, d3), F32),
        grid=(depth, d3 // d),
        in_specs=[pl.BlockSpec((rows, d), lambda l, j: (0, 0)),
                  pl.BlockSpec((None, d, d), lambda l, j: (l, 0, j)),
                  pl.BlockSpec((None, 1, d), lambda l, j: (l, 0, j))],
        out_specs=pl.BlockSpec((None, rows, d), lambda l, j: (l, 0, j)),
        compiler_params=_cparams(2),
        name="adaln_mod",
    )(c_all, w_ada, b_ada.reshape(depth, 1, d3))


def _lru_coeffs(xh, gh, k2):
    hb = xh.shape[1]
    t_r = jnp.tanh(gh[:, :hb])
    t_i = jnp.tanh(gh[:, hb:])
    a = jnp.exp2((t_r + 1.0) * k2)
    y = 1.0 - a * a
    mult = y * lax.rsqrt(jnp.maximum(y, TINY))
    u = (t_i + 1.0) * xh
    return a, mult * u, u


def _decay_const(lam_ref):
    return (-0.5 * LRU_C * LOG2E) * _softplus(-lam_ref[...])


def _gate_dot(xhb, ones, wg):
    return jnp.dot(jnp.concatenate([xhb, ones], axis=1), wg, preferred_element_type=F32)


def _ones_column(n_rows, width):
    lane = lax.broadcasted_iota(jnp.int32, (n_rows, width), 1)
    return jnp.where(lane == 0, 1.0, 0.0).astype(BF16)


def _segment_ends(a_ref, b_ref, reverse):
    g = SUBLANES
    n_groups = a_ref.shape[0] // g
    d = a_ref.shape[1]

    def body(s, hp):
        l = (n_groups - 1 - s) if reverse else s
        r0 = pl.multiple_of(l * g, g)
        a = a_ref[pl.ds(r0, g), :]
        return (a * hp[0] + b_ref[pl.ds(r0, g), :], a * hp[1])

    return lax.fori_loop(0, n_groups, body, (jnp.zeros((g, d), F32), jnp.ones((g, d), F32)), unroll=8)


def _segment_apply(a_ref, b_ref, c_in, out_ref, reverse):
    g = SUBLANES
    n_pairs = a_ref.shape[0] // (2 * g)

    def body(s, h):
        p = (n_pairs - 1 - s) if reverse else s
        r0 = pl.multiple_of(p * 2 * g, 2 * g)
        a = a_ref[pl.ds(r0, 2 * g), :]
        b = b_ref[pl.ds(r0, 2 * g), :]
        if reverse:
            h1 = a[g:] * h + b[g:]
            h0 = a[:g] * h1 + b[:g]
            h = h0
        else:
            h0 = a[:g] * h + b[:g]
            h1 = a[g:] * h0 + b[g:]
            h = h1
        out_ref[pl.ds(r0, 2 * g), :] = jnp.concatenate([h0, h1], axis=0).astype(out_ref.dtype)
        return h

    lax.fori_loop(0, n_pairs, body, c_in, unroll=4)


def _segment_carries(h_end, p_end, carry, reverse):
    rows = [None] * SUBLANES
    c = carry
    for s in (range(SUBLANES - 1, -1, -1) if reverse else range(SUBLANES)):
        rows[s] = c
        c = h_end[s:s + 1, :] + p_end[s:s + 1, :] * c
    return jnp.concatenate(rows, axis=0), c


def _lru_fwd_kernel(*refs, n_tiles, natural_in):
    (x_ref, xn_ref, mod_ref, g_ref, wxa_ref, cw_ref, cb_ref, wg_ref, lam_ref, wgb_ref,
     xc_ref, hf_ref, hbf_ref, hs_ref) = refs[:14]
    xs_ref = refs[14] if natural_in else None
    ext_ref, a_ref, b_ref, carry_ref, prev_ref, u_ref = refs[-6:]
    i = pl.program_id(1)
    g = SUBLANES
    d = g_ref.shape[1]
    n_rows = a_ref.shape[0]
    n_groups = n_rows // g
    hb = d // N_LRU_HEADS
    shift, scale, _ = _mod_rows(mod_ref, d)
    g1s = g_ref[...] * (1.0 + scale)

    @pl.when(i == 0)
    def _():
        carry_ref[...] = jnp.zeros_like(carry_ref)
        prev_ref[...] = jnp.zeros_like(prev_ref)

    _norm_to_rows(x_ref, hbf_ref, g1s, shift, n_groups, xs_ref)
    xn = xn_ref[:, 0, :] if natural_in else xn_ref[...]
    hbf_ref[n_rows:n_rows + 2 * g, :] = _norm_mod(jnp.concatenate([xn, xn], axis=0), g1s, shift).astype(BF16)
    row = lax.broadcasted_iota(jnp.int32, (g, XA_COLS), 0)
    nxt_scale = jnp.where(i == n_tiles - 1, 0.0, 1.0)

    def xa_block(k):
        cs = slice(k * XA_COLS, (k + 1) * XA_COLS)
        ext_ref[CONV_LEFT * g:, cs] = jnp.dot(hbf_ref[...], wxa_ref[:, cs], preferred_element_type=F32)
        last1 = ext_ref[(n_groups + 1) * g:(n_groups + 2) * g, cs]
        last2 = ext_ref[n_groups * g:(n_groups + 1) * g, cs]
        first = ext_ref[CONV_LEFT * g:(CONV_LEFT + 1) * g, cs]
        nxt = ext_ref[(n_groups + 2) * g:(n_groups + 3) * g, cs]
        ext_ref[g:2 * g, cs] = jnp.where(row == 0, pltpu.roll(prev_ref[g:2 * g, cs], 1, axis=0),
                                         pltpu.roll(last1, 1, axis=0))
        ext_ref[0:g, cs] = jnp.where(row == 0, pltpu.roll(prev_ref[0:g, cs], 1, axis=0),
                                     pltpu.roll(last2, 1, axis=0))
        prev_ref[0:g, cs] = last2
        prev_ref[g:2 * g, cs] = last1
        ext_ref[(n_groups + 2) * g:(n_groups + 3) * g, cs] = jnp.where(
            row == g - 1, pltpu.roll(nxt, g - 1, axis=0) * nxt_scale, pltpu.roll(first, g - 1, axis=0))

    k2 = _decay_const(lam_ref)
    ones = _ones_column(n_rows, hb)
    hcw = 0.5 * cw_ref[...]
    hcb = 0.5 * cb_ref[...]
    heads_per_block = XA_COLS // hb
    xa_block(0)
    for h in range(N_LRU_HEADS):
        if h % heads_per_block == 0:
            if (h // heads_per_block + 1) * XA_COLS < d:
                xa_block(h // heads_per_block + 1)
            c0 = h * hb
            sgb = _silu(jnp.dot(hbf_ref[0:n_rows, :], wgb_ref[:, c0:c0 + XA_COLS], preferred_element_type=F32))
            w = _pack_pair(hbf_ref[0:n_rows, c0:c0 + XA_COLS].astype(F32), sgb)
            for l in range(n_groups):
                hs_ref[:, l * d + c0:l * d + c0 + XA_COLS] = w[l * g:(l + 1) * g, :]
        sl = slice(h * hb, (h + 1) * hb)
        xh = ext_ref[0:n_rows, sl] * hcw[0:1, sl] + hcb[:, sl]
        for kk in range(1, CONV_WIDTH):
            xh = xh + ext_ref[kk * g:kk * g + n_rows, sl] * hcw[kk:kk + 1, sl]
        xhb = xh.astype(BF16)
        xc_ref[:, sl] = xhb
        a, b, u = _lru_coeffs(xh, _gate_dot(xhb, ones, wg_ref[h]), k2[:, sl])
        a_ref[:, sl] = a
        b_ref[:, sl] = b
        u_ref[:, sl] = u[0:g, :]

    @pl.when(i == 0)
    def _():
        first_row = lax.broadcasted_iota(jnp.int32, (g, d), 0) == 0
        b_ref[0:g, :] = jnp.where(first_row, u_ref[...], b_ref[0:g, :])

    h_end, p_end = _segment_ends(a_ref, b_ref, False)
    c_in, carry_ref[...] = _segment_carries(h_end, p_end, carry_ref[...], False)
    _segment_apply(a_ref, b_ref, c_in, hf_ref, False)


def _lru_fwd(x, mod, norm_g, w_xa, conv_w, conv_b, wg, lam, w_gb, *, natural_in):
    g = SUBLANES
    if natural_in:
        bsz, s, d = x.shape
        r = s // TOK
        x = x.reshape(bsz, r, TOK, d)
        x_spec = pl.BlockSpec((None, g, TOK, d), lambda b, i: (b, i, 0, 0))
        xn_spec = pl.BlockSpec((None, g, g, d), lambda b, i: (b, jnp.minimum(i + 1, n_tiles - 1), 0, 0))
    else:
        bsz, r, ld = x.shape
        d = ld // TOK
        x_spec = pl.BlockSpec((None, g, TOK * d), lambda b, i: (b, i, 0))
        xn_spec = pl.BlockSpec((None, g, d), lambda b, i: (b, jnp.minimum(i + 1, n_tiles - 1), 0))
    n_tiles = r // g
    n_rows = TOK * g
    hb = d // N_LRU_HEADS
    const = lambda b, i: (0, 0)
    rowform = jax.ShapeDtypeStruct((bsz, n_tiles, n_rows, d), BF16)
    rowspec = pl.BlockSpec((None, None, n_rows, d), lambda b, i: (b, i, 0, 0))
    storage = jax.ShapeDtypeStruct((bsz, r, TOK * d), F32)
    storage_spec = pl.BlockSpec((None, g, TOK * d), lambda b, i: (b, i, 0))
    out_shape = [rowform, rowform, jax.ShapeDtypeStruct((bsz, n_tiles, n_rows + 2 * g, d), BF16),
                 jax.ShapeDtypeStruct((bsz, r, TOK * d), jnp.uint32)]
    out_specs = [rowspec, rowspec, pl.BlockSpec((None, None, n_rows + 2 * g, d), lambda b, i: (b, i, 0, 0)),
                 storage_spec]
    if natural_in:
        out_shape.append(storage)
        out_specs.append(storage_spec)
    return pl.pallas_call(
        functools.partial(_lru_fwd_kernel, n_tiles=n_tiles, natural_in=natural_in),
        out_shape=tuple(out_shape),
        grid=(bsz, n_tiles),
        in_specs=[x_spec, xn_spec,
                  pl.BlockSpec((None, 1, 3 * d), lambda b, i: (b, 0, 0)),
                  _resident((1, d), const),
                  _resident((d, d), const),
                  _resident((CONV_WIDTH, d), const),
                  _resident((1, d), const),
                  _resident((N_LRU_HEADS, 2 * hb, 2 * hb), lambda b, i: (0, 0, 0)),
                  _resident((1, d), const),
                  _resident((d, d), const)],
        out_specs=tuple(out_specs),
        scratch_shapes=[pltpu.VMEM((n_rows + (CONV_LEFT + 2) * g, d), F32),
                        pltpu.VMEM((n_rows, d), F32),
                        pltpu.VMEM((n_rows, d), F32),
                        pltpu.VMEM((1, d), F32),
                        pltpu.VMEM((CONV_LEFT * g, d), F32),
                        pltpu.VMEM((g, d), F32)],
        compiler_params=_cparams(2),
        name="lru_fwd",
    )(x, x, mod, norm_g, w_xa, conv_w, conv_b, wg, lam, w_gb)


def _lru_bwd_kernel(xc_ref, hf_ref, hbf_ref, wga_ref, wg_ref, lam_ref, wao_ref,
                    ya_ref, a_ref, b_ref, sg_ref, carry_ref, u_ref):
    i = pl.program_id(1)
    g = SUBLANES
    n_rows, d = a_ref.shape
    hb = d // N_LRU_HEADS

    @pl.when(i == 0)
    def _():
        carry_ref[...] = jnp.zeros_like(carry_ref)

    k2 = _decay_const(lam_ref)
    ones = _ones_column(n_rows, hb)
    heads_per_block = XA_COLS // hb
    for h in range(N_LRU_HEADS):
        if h % heads_per_block == 0:
            cs = slice(h * hb, h * hb + XA_COLS)
            sg_ref[:, cs] = _silu(jnp.dot(hbf_ref[0:n_rows, :], wga_ref[:, cs], preferred_element_type=F32))
        sl = slice(h * hb, (h + 1) * hb)
        xhb = xc_ref[:, sl]
        a, b, u = _lru_coeffs(xhb.astype(F32), _gate_dot(xhb, ones, wg_ref[h]), k2[:, sl])
        a_ref[:, sl] = a
        b_ref[:, sl] = b
        u_ref[:, sl] = u[n_rows - g:, :]

    @pl.when(i == 0)
    def _():
        row = lax.broadcasted_iota(jnp.int32, (g, d), 0)
        b_ref[n_rows - g:, :] = jnp.where(row == g - 1, u_ref[...], b_ref[n_rows - g:, :])

    h_end, p_end = _segment_ends(a_ref, b_ref, True)
    c_in, carry_ref[...] = _segment_carries(h_end, p_end, carry_ref[...], True)
    _segment_apply(a_ref, b_ref, c_in, b_ref, True)

    for r0 in range(0, n_rows, ROW_CHUNK):
        rs = slice(r0, r0 + ROW_CHUNK)
        ya = (hf_ref[rs, :].astype(F32) + b_ref[rs, :]) * sg_ref[rs, :]
        out = jnp.dot(ya.astype(BF16), wao_ref[...], preferred_element_type=F32)
        for gl in range(ROW_CHUNK // g):
            l = r0 // g + gl
            ya_ref[:, l * d:(l + 1) * d] = out[gl * g:(gl + 1) * g, :]


def _lru_bwd(xc, hf, hbf, w_ga, wg, lam, w_a_out):
    bsz, n_tiles, n_rows, d = xc.shape
    g = SUBLANES
    hb = d // N_LRU_HEADS
    const = lambda b, i: (0, 0)
    rev3 = lambda b, i: (b, n_tiles - 1 - i, 0)
    rev4 = lambda b, i: (b, n_tiles - 1 - i, 0, 0)
    return pl.pallas_call(
        _lru_bwd_kernel,
        out_shape=jax.ShapeDtypeStruct((bsz, n_tiles * g, TOK * d), F32),
        grid=(bsz, n_tiles),
        in_specs=[pl.BlockSpec((None, None, n_rows, d), rev4),
                  pl.BlockSpec((None, None, n_rows, d), rev4),
                  pl.BlockSpec((None, None, n_rows + 2 * g, d), rev4),
                  _resident((d, d), const),
                  _resident((N_LRU_HEADS, 2 * hb, 2 * hb), lambda b, i: (0, 0, 0)),
                  _resident((1, d), const),
                  _resident((d, d), const)],
        out_specs=pl.BlockSpec((None, g, TOK * d), rev3),
        scratch_shapes=[pltpu.VMEM((n_rows, d), F32),
                        pltpu.VMEM((n_rows, d), F32),
                        pltpu.VMEM((n_rows, d), F32),
                        pltpu.VMEM((1, d), F32),
                        pltpu.VMEM((g, d), F32)],
        compiler_params=_cparams(2),
        name="lru_bwd",
    )(xc, hf, hbf, w_ga, wg, lam, w_a_out)


def _fft_s1_kernel(*refs, tn, halves):
    x_refs = refs[:halves]
    tw_refs = refs[halves:2 * halves]
    wxb_ref, fs1_ref, o_ref = refs[2 * halves:]
    r = x_refs[0].shape[0]
    d = wxb_ref.shape[0]
    hn_all = jnp.concatenate(
        [_unpack_pair(x_refs[hf][:, j * d:(j + 1) * d])[0].astype(BF16)
         for hf in range(halves) for j in range(tn)], axis=0)
    xb_all = jnp.dot(hn_all, wxb_ref[...], preferred_element_type=F32).astype(BF16)
    words = [[] for _ in range(halves)]
    for j in range(tn):
        parts = []
        for hf in range(halves):
            c0 = (hf * tn + j) * r
            pq = jnp.dot(fs1_ref[...], xb_all[c0:c0 + r, :], preferred_element_type=F32)
            p, q = pq[:r], pq[r:]
            tc = tw_refs[hf][:, j:j + 1]
            ts = tw_refs[hf][:, tn + j:tn + j + 1]
            parts.append((p * tc - q * ts, p * ts + q * tc))
        if halves == 1:
            words[0].append(_pack_pair(*parts[0]))
        else:
            words[0].append(_pack_pair(parts[0][0] + parts[1][0], parts[0][1] + parts[1][1]))
            words[1].append(_pack_pair(parts[0][0] - parts[1][0], parts[0][1] - parts[1][1]))
    for k2 in range(halves):
        o_ref[k2] = jnp.swapaxes(jnp.stack(words[k2], axis=0), 0, 1)


def _fft_s1(x, w_xb, fs1, tw, *, tn, halves):
    bsz, r, ld = x.shape
    d = ld // TOK
    nc = TOK // halves
    nblk = nc // tn
    const = lambda b, i: (0, 0)
    x_specs = [pl.BlockSpec((None, r, tn * d), functools.partial(lambda b, i, hf: (b, 0, hf * nblk + i), hf=hf))
               for hf in range(halves)]
    tw_specs = [pl.BlockSpec((None, r, 2 * tn), functools.partial(lambda b, i, hf: (hf * nblk + i, 0, 0), hf=hf))
                for hf in range(halves)]
    return pl.pallas_call(
        functools.partial(_fft_s1_kernel, tn=tn, halves=halves),
        out_shape=jax.ShapeDtypeStruct((bsz, halves, r, nc, d), jnp.uint32),
        grid=(bsz, nblk),
        in_specs=x_specs + tw_specs + [
            _resident((d, d), const),
            _resident((2 * r, r), const)],
        out_specs=pl.BlockSpec((None, halves, r, tn, d), lambda b, i: (b, 0, 0, i, 0)),
        compiler_params=_cparams(2),
        name="fft_s1",
    )(*([x] * halves), *([tw] * halves), w_xb, fs1)


def _fft_s2_kernel(a_ref, x_ref, h_ref, ya_ref, mod_ref, wm_ref, m2_ref, cc_ref, cs_ref, wbo_ref, wo_ref,
                   fg_ref, o_ref, *, tk1, final):
    r = x_ref.shape[0]
    d = fg_ref.shape[1]
    gw = d // N_FOURIER_GROUPS
    _, _, gate = _mod_rows(mod_ref, d)
    cols = [slice(q * d, (q + 1) * d) for q in range(tk1)]

    us = []
    for q in range(tk1):
        re, im = _unpack_pair(a_ref[q])
        rhs = jnp.concatenate([re.astype(BF16), im.astype(BF16)], axis=0)
        us.append(jnp.dot(m2_ref[...], rhs, preferred_element_type=F32))
    ur = jnp.concatenate([u[:r] for u in us], axis=0).astype(BF16)
    uq = jnp.concatenate([u[r:] for u in us], axis=0).astype(BF16)
    x = jnp.concatenate([x_ref[:, cs] for cs in cols], axis=0)
    ya = jnp.concatenate([ya_ref[:, cs] for cs in cols], axis=0)

    hn, sgb = _unpack_pair(jnp.concatenate([h_ref[:, cs] for cs in cols], axis=0))
    gm = jnp.dot(hn.astype(BF16), wm_ref[...], preferred_element_type=F32)
    y = jnp.concatenate(
        [jnp.dot(ur[:, gi * gw:(gi + 1) * gw], cc_ref[...], preferred_element_type=F32)
         + jnp.dot(uq[:, gi * gw:(gi + 1) * gw], cs_ref[...], preferred_element_type=F32)
         for gi in range(N_FOURIER_GROUPS)], axis=1)
    yb = jnp.dot((y * sgb).astype(BF16), wbo_ref[...], preferred_element_type=F32)
    mix = _sigmoid(gm[:, :d]) * ya + _sigmoid(gm[:, d:]) * yb
    out = jnp.dot(mix.astype(BF16), wo_ref[...], preferred_element_type=F32)
    xn = x + gate * out
    if final:
        ms = jnp.mean(xn * xn, axis=-1, keepdims=True)
        xn = xn * lax.rsqrt(ms + EPS) * fg_ref[...]
    for q, cs in enumerate(cols):
        if len(o_ref.shape) == 3:
            o_ref[:, q, :] = xn[q * r:(q + 1) * r, :]
        else:
            o_ref[:, cs] = xn[q * r:(q + 1) * r, :]


def _fft_s2(a, x, h, ya, mod, w_m, m2, cc, cs, w_b_out, w_o, final_g, *, tk1, final):
    bsz, r, ld = x.shape
    d = ld // TOK
    halves = a.shape[1]
    nc = a.shape[3]
    nblk = (TOK // halves) // tk1
    const = lambda b, k2, i: (0, 0)
    tok = pl.BlockSpec((None, r, tk1 * d), lambda b, k2, i: (b, 0, k2 * nblk + i))
    gw = d // N_FOURIER_GROUPS
    if final and tk1 % SUBLANES == 0:
        out_shape = jax.ShapeDtypeStruct((bsz, r, TOK, d), F32)
        out_spec = pl.BlockSpec((None, r, tk1, d), lambda b, k2, i: (b, 0, k2 * nblk + i, 0))
    else:
        out_shape = jax.ShapeDtypeStruct((bsz, r, ld), F32)
        out_spec = tok
    return pl.pallas_call(
        functools.partial(_fft_s2_kernel, tk1=tk1, final=final),
        out_shape=out_shape,
        grid=(bsz, halves, nblk),
        in_specs=[pl.BlockSpec((None, None, tk1, nc, d), lambda b, k2, i: (b, k2, i, 0, 0)),
                  tok, tok, tok,
                  pl.BlockSpec((None, 1, 3 * d), lambda b, k2, i: (b, 0, 0)),
                  _resident((d, 2 * d), const),
                  pl.BlockSpec((None, 2 * nc, 2 * nc), lambda b, k2, i: (k2, 0, 0)),
                  _resident((gw, gw), const),
                  _resident((gw, gw), const),
                  _resident((d, d), const),
                  _resident((d, d), const),
                  _resident((1, d), const)],
        out_specs=out_spec,
        compiler_params=_cparams(3),
        name="fft_s2",
    )(a, x, h, ya, mod, w_m, m2, cc, cs, w_b_out, w_o, final_g)


def _dft_tables(r, tn, gw):
    halves = TOK // r
    s = r * TOK
    k1 = np.arange(r)
    ang1 = 2.0 * np.pi * np.outer(k1, k1) / r
    fs1 = np.concatenate([np.cos(ang1), np.sin(ang1)], axis=0)
    tok = np.arange(TOK)
    ang_t = 2.0 * np.pi * np.outer(k1, tok) / s
    scale = 1.0 / np.sqrt(float(s) * gw)
    tc = (np.cos(ang_t) * scale).reshape(r, TOK // tn, tn).transpose(1, 0, 2)
    ts = (np.sin(ang_t) * scale).reshape(r, TOK // tn, tn).transpose(1, 0, 2)
    tw = np.concatenate([tc, ts], axis=2)
    n = np.arange(r)
    m2 = []
    for k2 in range(halves):
        ang2 = 2.0 * np.pi * np.outer(halves * n + k2, n) / (halves * r)
        c2, s2 = np.cos(ang2), np.sin(ang2)
        m2.append(np.block([[c2, -s2], [s2, c2]]))
    c = np.arange(gw)
    angc = 2.0 * np.pi * np.outer(c, c) / gw
    return (jnp.asarray(fs1, BF16), jnp.asarray(tw, F32), jnp.asarray(np.stack(m2), BF16),
            jnp.asarray(np.cos(angc), BF16), jnp.asarray(-np.sin(angc), BF16))


def _trunk(x, mods, lw, final_g):
    bsz, s, d = x.shape
    r = s // TOK
    halves = TOK // r
    assert r * TOK == s and r * halves == TOK and halves in (1, 2) and r % SUBLANES == 0
    tn = 8
    tk1 = 512 // r
    gw = d // N_FOURIER_GROUPS
    fs1, tw, m2, cc, cs = _dft_tables(r, tn, gw)
    depth = len(lw)
    for l in range(depth):
        w = lw[l]
        mod = mods[l]
        res = _lru_fwd(x, mod, w["norm_g"], w["w_xa"], w["conv_w"], w["conv_b"], w["wg"][0], w["lam"][0],
                       w["w_gb"], natural_in=(l == 0))
        xc, hf, hbf, h = res[:4]
        if l == 0:
            x = res[4]
        ya = _lru_bwd(xc, hf, hbf, w["w_ga"], w["wg"][1], w["lam"][1], w["w_a_out"])
        a = _fft_s1(h, w["w_xb"], fs1, tw, tn=tn, halves=halves)
        x = _fft_s2(a, x, h, ya, mod, w["w_m"], m2, cc, cs, w["w_b_out"], w["w_o"], final_g,
                    tk1=tk1, final=(l == depth - 1))
    return x.reshape(bsz, s, d)


def _gate_weights(w, b):
    n_heads, blk = w.shape[1], w.shape[2]
    bias = 0.5 * jnp.concatenate([b[0].reshape(n_heads, 1, blk), b[1].reshape(n_heads, 1, blk)], axis=-1)
    rows = jnp.concatenate([jnp.concatenate([w[0], w[1]], axis=-1), bias,
                            jnp.zeros((n_heads, blk - 1, 2 * blk), F32)], axis=1)
    return rows.astype(BF16)


def _layer_weights(norm_g, w_in, conv_w, conv_b, w_rg, b_rg, lam, w_a_out, w_b_out, w_o):
    depth, d = norm_g.shape
    lw = []
    for l in range(depth):
        wi = w_in[l].astype(BF16)
        lw.append(dict(
            norm_g=norm_g[l].reshape(1, d),
            w_xa=wi[:, 0:d], w_ga=wi[:, d:2 * d], w_xb=wi[:, 2 * d:3 * d], w_gb=wi[:, 3 * d:4 * d],
            w_m=wi[:, 4 * d:6 * d],
            conv_w=conv_w[l], conv_b=conv_b[l].reshape(1, d),
            wg=[_gate_weights(w_rg[l, dr], b_rg[l, dr]) for dr in range(2)],
            lam=[lam[l, dr].reshape(1, d) for dr in range(2)],
            w_a_out=w_a_out[l].astype(BF16), w_b_out=w_b_out[l].astype(BF16), w_o=w_o[l].astype(BF16)))
    return lw


def kernel(x_prompt, x_sample, c_prompt, c_sample, norm_g, w_ada, b_ada, w_in, conv_w, conv_b, w_rg, b_rg, lam,
           w_a_out, w_b_out, w_o, final_g):
    d = x_prompt.shape[-1]
    depth = norm_g.shape[0]
    bp, bs = c_prompt.shape[0], c_sample.shape[0]
    rows = -(-(bp + bs) // SUBLANES) * SUBLANES
    c_all = jnp.concatenate([c_prompt, c_sample, jnp.zeros((rows - bp - bs, d), F32)], axis=0)
    mod = _adaln_mod(c_all, w_ada, b_ada)
    mods_p = [mod[l, 0:bp].reshape(bp, 1, 3 * d) for l in range(depth)]
    mods_s = [mod[l, bp:bp + bs].reshape(bs, 1, 3 * d) for l in range(depth)]
    lw = _layer_weights(norm_g, w_in, conv_w, conv_b, w_rg, b_rg, lam, w_a_out, w_b_out, w_o)
    fg = final_g.reshape(1, d)
    return (_trunk(x_prompt, mods_p, lw, fg), _trunk(x_sample, mods_s, lw, fg))
```

```python
import functools

import numpy as np
import jax
import jax.numpy as jnp
from jax import lax
from jax.experimental import pallas as pl
from jax.experimental.pallas import tpu as pltpu

F32 = jnp.float32
BF16 = jnp.bfloat16

N_LRU_HEADS = 8
N_FOURIER_GROUPS = 4
CONV_WIDTH = 4
CONV_LEFT = 2
LRU_C = 8.0
EPS = 1e-6

SUBLANES = 8
TOK = 128
ROW_CHUNK = 512
XA_COLS = 256
XB_COLS = 256
VMEM_LIMIT = 58 * 1024 * 1024
TINY = 1e-30
LOG2E = 1.4426950408889634


def _cparams(n_axes):
    return pltpu.CompilerParams(dimension_semantics=("arbitrary",) * n_axes,
                                vmem_limit_bytes=VMEM_LIMIT)


def _resident(shape, index_map):
    return pl.BlockSpec(shape, index_map, pipeline_mode=pl.Buffered(1))


def _sigmoid(z):
    return 0.5 * jnp.tanh(0.5 * z) + 0.5


def _silu(z):
    zh = 0.5 * z
    return zh * (jnp.tanh(zh) + 1.0)


def _softplus(z):
    return jnp.maximum(z, 0.0) + jnp.log1p(jnp.exp(-jnp.abs(z)))


def _norm_mod(x, g1s, shift):
    ms = jnp.mean(x * x, axis=-1, keepdims=True)
    return x * lax.rsqrt(ms + EPS) * g1s + shift


def _pack_pair(hi, lo):
    h = lax.bitcast_convert_type(hi.astype(BF16).astype(F32), jnp.uint32)
    l = lax.bitcast_convert_type(lo.astype(BF16).astype(F32), jnp.uint32)
    return h | (l >> 16)


def _unpack_pair(w):
    return (lax.bitcast_convert_type(w & jnp.uint32(0xFFFF0000), F32),
            lax.bitcast_convert_type(w << 16, F32))


def _mod_rows(mod_ref, d):
    return mod_ref[:, 0:d], mod_ref[:, d:2 * d], mod_ref[:, 2 * d:3 * d]


def _norm_to_rows(x_ref, hbf_ref, g1s, shift, n_groups, xs_ref=None):
    d = g1s.shape[1]
    g = SUBLANES

    def group(l):
        if xs_ref is None:
            return x_ref[:, l * d:(l + 1) * d]
        xg = x_ref[:, l, :]
        xs_ref[:, l * d:(l + 1) * d] = xg
        return xg

    for l in range(0, n_groups, 2):
        xx = jnp.concatenate([group(l), group(l + 1)], axis=0)
        hbf_ref[l * g:(l + 2) * g, :] = _norm_mod(xx, g1s, shift).astype(BF16)


def _mod_kernel(c_ref, w_ref, b_ref, o_ref):
    c = c_ref[...]
    o_ref[...] = jnp.dot(_silu(c), w_ref[...], preferred_element_type=F32,
                         precision=lax.Precision.HIGHEST) + b_ref[...]


def _adaln_mod(c_all, w_ada, b_ada):
    depth, d, d3 = w_ada.shape
    rows = c_all.shape[0]
    return pl.pallas_call(
        _mod_kernel,
        out_shape=jax.ShapeDtypeStruct((depth, rows, d3), F32),
        grid=(depth, d3 // d),
        in_specs=[pl.BlockSpec((rows, d), lambda l, j: (0, 0)),
                  pl.BlockSpec((None, d, d), lambda l, j: (l, 0, j)),
                  pl.BlockSpec((None, 1, d), lambda l, j: (l, 0, j))],
        out_specs=pl.BlockSpec((None, rows, d), lambda l, j: (l, 0, j)),
        compiler_params=_cparams(2),
        name="adaln_mod",
    )(c_all, w_ada, b_ada.reshape(depth, 1, d3))


def _lru_coeffs(xh, gh, k2):
    hb = xh.shape[1]
    t_r = jnp.tanh(gh[:, :hb])
    t_i = jnp.tanh(gh[:, hb:])
    a = jnp.exp2((t_r + 1.0) * k2)
    y = 1.0 - a * a
    mult = y * lax.rsqrt(jnp.maximum(y, TINY))
    u = (t_i + 1.0) * xh
    return a, mult * u, u


def _decay_const(lam_ref):
    return (-0.5 * LRU_C * LOG2E) * _softplus(-lam_ref[...])


def _gate_dot(xhb, ones, wg):
    return jnp.dot(jnp.concatenate([xhb, ones], axis=1), wg, preferred_element_type=F32)


def _ones_column(n_rows, width):
    lane = lax.broadcasted_iota(jnp.int32, (n_rows, width), 1)
    return jnp.where(lane == 0, 1.0, 0.0).astype(BF16)


def _segment_ends(a_ref, b_ref, reverse):
    g = SUBLANES
    n_groups = a_ref.shape[0] // g
    d = a_ref.shape[1]

    def body(s, hp):
        l = (n_groups - 1 - s) if reverse else s
        r0 = pl.multiple_of(l * g, g)
        a = a_ref[pl.ds(r0, g), :]
        return (a * hp[0] + b_ref[pl.ds(r0, g), :], a * hp[1])

    return lax.fori_loop(0, n_groups, body, (jnp.zeros((g, d), F32), jnp.ones((g, d), F32)), unroll=8)


def _segment_apply(a_ref, b_ref, c_in, out_ref, reverse):
    g = SUBLANES
    n_pairs = a_ref.shape[0] // (2 * g)

    def body(s, h):
        p = (n_pairs - 1 - s) if reverse else s
        r0 = pl.multiple_of(p * 2 * g, 2 * g)
        a = a_ref[pl.ds(r0, 2 * g), :]
        b = b_ref[pl.ds(r0, 2 * g), :]
        if reverse:
            h1 = a[g:] * h + b[g:]
            h0 = a[:g] * h1 + b[:g]
            h = h0
        else:
            h0 = a[:g] * h + b[:g]
            h1 = a[g:] * h0 + b[g:]
            h = h1
        out_ref[pl.ds(r0, 2 * g), :] = jnp.concatenate([h0, h1], axis=0).astype(out_ref.dtype)
        return h

    lax.fori_loop(0, n_pairs, body, c_in, unroll=4)


def _segment_carries(h_end, p_end, carry, reverse):
    rows = [None] * SUBLANES
    c = carry
    for s in (range(SUBLANES - 1, -1, -1) if reverse else range(SUBLANES)):
        rows[s] = c
        c = h_end[s:s + 1, :] + p_end[s:s + 1, :] * c
    return jnp.concatenate(rows, axis=0), c


def _lru_fwd_kernel(*refs, n_tiles, natural_in):
    (x_ref, xn_ref, mod_ref, g_ref, wxa_ref, cw_ref, cb_ref, wg_ref, lam_ref, wgb_ref,
     xc_ref, hf_ref, hbf_ref, hs_ref) = refs[:14]
    xs_ref = refs[14] if natural_in else None
    ext_ref, a_ref, b_ref, carry_ref, prev_ref, u_ref = refs[-6:]
    i = pl.program_id(1)
    g = SUBLANES
    d = g_ref.shape[1]
    n_rows = a_ref.shape[0]
    n_groups = n_rows // g
    hb = d // N_LRU_HEADS
    shift, scale, _ = _mod_rows(mod_ref, d)
    g1s = g_ref[...] * (1.0 + scale)

    @pl.when(i == 0)
    def _():
        carry_ref[...] = jnp.zeros_like(carry_ref)
        prev_ref[...] = jnp.zeros_like(prev_ref)

    _norm_to_rows(x_ref, hbf_ref, g1s, shift, n_groups, xs_ref)
    xn = xn_ref[:, 0, :] if natural_in else xn_ref[...]
    hbf_ref[n_rows:n_rows + 2 * g, :] = _norm_mod(jnp.concatenate([xn, xn], axis=0), g1s, shift).astype(BF16)
    row = lax.broadcasted_iota(jnp.int32, (g, XA_COLS), 0)
    nxt_scale = jnp.where(i == n_tiles - 1, 0.0, 1.0)

    def xa_block(k):
        cs = slice(k * XA_COLS, (k + 1) * XA_COLS)
        ext_ref[CONV_LEFT * g:, cs] = jnp.dot(hbf_ref[...], wxa_ref[:, cs], preferred_element_type=F32)
        last1 = ext_ref[(n_groups + 1) * g:(n_groups + 2) * g, cs]
        last2 = ext_ref[n_groups * g:(n_groups + 1) * g, cs]
        first = ext_ref[CONV_LEFT * g:(CONV_LEFT + 1) * g, cs]
        nxt = ext_ref[(n_groups + 2) * g:(n_groups + 3) * g, cs]
        ext_ref[g:2 * g, cs] = jnp.where(row == 0, pltpu.roll(prev_ref[g:2 * g, cs], 1, axis=0),
                                         pltpu.roll(last1, 1, axis=0))
        ext_ref[0:g, cs] = jnp.where(row == 0, pltpu.roll(prev_ref[0:g, cs], 1, axis=0),
                                     pltpu.roll(last2, 1, axis=0))
        prev_ref[0:g, cs] = last2
        prev_ref[g:2 * g, cs] = last1
        ext_ref[(n_groups + 2) * g:(n_groups + 3) * g, cs] = jnp.where(
            row == g - 1, pltpu.roll(nxt, g - 1, axis=0) * nxt_scale, pltpu.roll(first, g - 1, axis=0))

    k2 = _decay_const(lam_ref)
    ones = _ones_column(n_rows, hb)
    hcw = 0.5 * cw_ref[...]
    hcb = 0.5 * cb_ref[...]
    heads_per_block = XA_COLS // hb
    xa_block(0)
    for h in range(N_LRU_HEADS):
        if h % heads_per_block == 0:
            if (h // heads_per_block + 1) * XA_COLS < d:
                xa_block(h // heads_per_block + 1)
            c0 = h * hb
            sgb = _silu(jnp.dot(hbf_ref[0:n_rows, :], wgb_ref[:, c0:c0 + XA_COLS], preferred_element_type=F32))
            w = _pack_pair(hbf_ref[0:n_rows, c0:c0 + XA_COLS].astype(F32), sgb)
            for l in range(n_groups):
                hs_ref[:, l * d + c0:l * d + c0 + XA_COLS] = w[l * g:(l + 1) * g, :]
        sl = slice(h * hb, (h + 1) * hb)
        xh = ext_ref[0:n_rows, sl] * hcw[0:1, sl] + hcb[:, sl]
        for kk in range(1, CONV_WIDTH):
            xh = xh + ext_ref[kk * g:kk * g + n_rows, sl] * hcw[kk:kk + 1, sl]
        xhb = xh.astype(BF16)
        xc_ref[:, sl] = xhb
        a, b, u = _lru_coeffs(xh, _gate_dot(xhb, ones, wg_ref[h]), k2[:, sl])
        a_ref[:, sl] = a
        b_ref[:, sl] = b
        u_ref[:, sl] = u[0:g, :]

    @pl.when(i == 0)
    def _():
        first_row = lax.broadcasted_iota(jnp.int32, (g, d), 0) == 0
        b_ref[0:g, :] = jnp.where(first_row, u_ref[...], b_ref[0:g, :])

    h_end, p_end = _segment_ends(a_ref, b_ref, False)
    c_in, carry_ref[...] = _segment_carries(h_end, p_end, carry_ref[...], False)
    _segment_apply(a_ref, b_ref, c_in, hf_ref, False)


def _lru_fwd(x, mod, norm_g, w_xa, conv_w, conv_b, wg, lam, w_gb, *, natural_in):
    g = SUBLANES
    if natural_in:
        bsz, s, d = x.shape
        r = s // TOK
        x = x.reshape(bsz, r, TOK, d)
        x_spec = pl.BlockSpec((None, g, TOK, d), lambda b, i: (b, i, 0, 0))
        xn_spec = pl.BlockSpec((None, g, g, d), lambda b, i: (b, jnp.minimum(i + 1, n_tiles - 1), 0, 0))
    else:
        bsz, r, ld = x.shape
        d = ld // TOK
        x_spec = pl.BlockSpec((None, g, TOK * d), lambda b, i: (b, i, 0))
        xn_spec = pl.BlockSpec((None, g, d), lambda b, i: (b, jnp.minimum(i + 1, n_tiles - 1), 0))
    n_tiles = r // g
    n_rows = TOK * g
    hb = d // N_LRU_HEADS
    const = lambda b, i: (0, 0)
    rowform = jax.ShapeDtypeStruct((bsz, n_tiles, n_rows, d), BF16)
    rowspec = pl.BlockSpec((None, None, n_rows, d), lambda b, i: (b, i, 0, 0))
    storage = jax.ShapeDtypeStruct((bsz, r, TOK * d), F32)
    storage_spec = pl.BlockSpec((None, g, TOK * d), lambda b, i: (b, i, 0))
    out_shape = [rowform, rowform, jax.ShapeDtypeStruct((bsz, n_tiles, n_rows + 2 * g, d), BF16),
                 jax.ShapeDtypeStruct((bsz, r, TOK * d), jnp.uint32)]
    out_specs = [rowspec, rowspec, pl.BlockSpec((None, None, n_rows + 2 * g, d), lambda b, i: (b, i, 0, 0)),
                 storage_spec]
    if natural_in:
        out_shape.append(storage)
        out_specs.append(storage_spec)
    return pl.pallas_call(
        functools.partial(_lru_fwd_kernel, n_tiles=n_tiles, natural_in=natural_in),
        out_shape=tuple(out_shape),
        grid=(bsz, n_tiles),
        in_specs=[x_spec, xn_spec,
                  pl.BlockSpec((None, 1, 3 * d), lambda b, i: (b, 0, 0)),
                  _resident((1, d), const),
                  _resident((d, d), const),
                  _resident((CONV_WIDTH, d), const),
                  _resident((1, d), const),
                  _resident((N_LRU_HEADS, 2 * hb, 2 * hb), lambda b, i: (0, 0, 0)),
                  _resident((1, d), const),
                  _resident((d, d), const)],
        out_specs=tuple(out_specs),
        scratch_shapes=[pltpu.VMEM((n_rows + (CONV_LEFT + 2) * g, d), F32),
                        pltpu.VMEM((n_rows, d), F32),
                        pltpu.VMEM((n_rows, d), F32),
                        pltpu.VMEM((1, d), F32),
                        pltpu.VMEM((CONV_LEFT * g, d), F32),
                        pltpu.VMEM((g, d), F32)],
        compiler_params=_cparams(2),
        name="lru_fwd",
    )(x, x, mod, norm_g, w_xa, conv_w, conv_b, wg, lam, w_gb)


def _lru_bwd_kernel(xc_ref, hf_ref, hbf_ref, wga_ref, wg_ref, lam_ref, wao_ref,
                    ya_ref, a_ref, b_ref, sg_ref, carry_ref, u_ref):
    i = pl.program_id(1)
    g = SUBLANES
    n_rows, d = a_ref.shape
    hb = d // N_LRU_HEADS

    @pl.when(i == 0)
    def _():
        carry_ref[...] = jnp.zeros_like(carry_ref)

    k2 = _decay_const(lam_ref)
    ones = _ones_column(n_rows, hb)
    heads_per_block = XA_COLS // hb
    for h in range(N_LRU_HEADS):
        if h % heads_per_block == 0:
            cs = slice(h * hb, h * hb + XA_COLS)
            sg_ref[:, cs] = _silu(jnp.dot(hbf_ref[0:n_rows, :], wga_ref[:, cs], preferred_element_type=F32))
        sl = slice(h * hb, (h + 1) * hb)
        xhb = xc_ref[:, sl]
        a, b, u = _lru_coeffs(xhb.astype(F32), _gate_dot(xhb, ones, wg_ref[h]), k2[:, sl])
        a_ref[:, sl] = a
        b_ref[:, sl] = b
        u_ref[:, sl] = u[n_rows - g:, :]

    @pl.when(i == 0)
    def _():
        row = lax.broadcasted_iota(jnp.int32, (g, d), 0)
        b_ref[n_rows - g:, :] = jnp.where(row == g - 1, u_ref[...], b_ref[n_rows - g:, :])

    h_end, p_end = _segment_ends(a_ref, b_ref, True)
    c_in, carry_ref[...] = _segment_carries(h_end, p_end, carry_ref[...], True)
    _segment_apply(a_ref, b_ref, c_in, b_ref, True)

    for r0 in range(0, n_rows, ROW_CHUNK):
        rs = slice(r0, r0 + ROW_CHUNK)
        ya = (hf_ref[rs, :].astype(F32) + b_ref[rs, :]) * sg_ref[rs, :]
        out = jnp.dot(ya.astype(BF16), wao_ref[...], preferred_element_type=F32)
        for gl in range(ROW_CHUNK // g):
            l = r0 // g + gl
            ya_ref[:, l * d:(l + 1) * d] = out[gl * g:(gl + 1) * g, :]


def _lru_bwd(xc, hf, hbf, w_ga, wg, lam, w_a_out):
    bsz, n_tiles, n_rows, d = xc.shape
    g = SUBLANES
    hb = d // N_LRU_HEADS
    const = lambda b, i: (0, 0)
    rev3 = lambda b, i: (b, n_tiles - 1 - i, 0)
    rev4 = lambda b, i: (b, n_tiles - 1 - i, 0, 0)
    return pl.pallas_call(
        _lru_bwd_kernel,
        out_shape=jax.ShapeDtypeStruct((bsz, n_tiles * g, TOK * d), F32),
        grid=(bsz, n_tiles),
        in_specs=[pl.BlockSpec((None, None, n_rows, d), rev4),
                  pl.BlockSpec((None, None, n_rows, d), rev4),
                  pl.BlockSpec((None, None, n_rows + 2 * g, d), rev4),
                  _resident((d, d), const),
                  _resident((N_LRU_HEADS, 2 * hb, 2 * hb), lambda b, i: (0, 0, 0)),
                  _resident((1, d), const),
                  _resident((d, d), const)],
        out_specs=pl.BlockSpec((None, g, TOK * d), rev3),
        scratch_shapes=[pltpu.VMEM((n_rows, d), F32),
                        pltpu.VMEM((n_rows, d), F32),
                        pltpu.VMEM((n_rows, d), F32),
                        pltpu.VMEM((1, d), F32),
                        pltpu.VMEM((g, d), F32)],
        compiler_params=_cparams(2),
        name="lru_bwd",
    )(xc, hf, hbf, w_ga, wg, lam, w_a_out)


def _fft_s1_kernel(*refs, tn, halves):
    x_refs = refs[:halves]
    tw_refs = refs[halves:2 * halves]
    wxb_ref, fs1_ref, o_ref = refs[2 * halves:]
    r = x_refs[0].shape[0]
    d = wxb_ref.shape[0]
    hn_all = jnp.concatenate(
        [_unpack_pair(x_refs[hf][:, j * d:(j + 1) * d])[0].astype(BF16)
         for hf in range(halves) for j in range(tn)], axis=0)
    def xb_block(k):
        return jnp.dot(hn_all, wxb_ref[:, k * XB_COLS:(k + 1) * XB_COLS], preferred_element_type=F32).astype(BF16)

    xb_next = xb_block(0)
    for k in range(d // XB_COLS):
        cs = slice(k * XB_COLS, (k + 1) * XB_COLS)
        xb, xb_next = xb_next, (xb_block(k + 1) if (k + 1) * XB_COLS < d else None)
        words = [[] for _ in range(halves)]
        for j in range(tn):
            parts = []
            for hf in range(halves):
                c0 = (hf * tn + j) * r
                pq = jnp.dot(fs1_ref[...], xb[c0:c0 + r, :], preferred_element_type=F32)
                p, q = pq[:r], pq[r:]
                tc = tw_refs[hf][:, j:j + 1]
                ts = tw_refs[hf][:, tn + j:tn + j + 1]
                parts.append((p * tc - q * ts, p * ts + q * tc))
            if halves == 1:
                words[0].append(_pack_pair(*parts[0]))
            else:
                words[0].append(_pack_pair(parts[0][0] + parts[1][0], parts[0][1] + parts[1][1]))
                words[1].append(_pack_pair(parts[0][0] - parts[1][0], parts[0][1] - parts[1][1]))
        for k2 in range(halves):
            o_ref[k2, :, :, cs] = jnp.swapaxes(jnp.stack(words[k2], axis=0), 0, 1)


def _fft_s1(x, w_xb, fs1, tw, *, tn, halves):
    bsz, r, ld = x.shape
    d = ld // TOK
    nc = TOK // halves
    nblk = nc // tn
    const = lambda b, i: (0, 0)
    x_specs = [pl.BlockSpec((None, r, tn * d), functools.partial(lambda b, i, hf: (b, 0, hf * nblk + i), hf=hf))
               for hf in range(halves)]
    tw_specs = [pl.BlockSpec((None, r, 2 * tn), functools.partial(lambda b, i, hf: (hf * nblk + i, 0, 0), hf=hf))
                for hf in range(halves)]
    return pl.pallas_call(
        functools.partial(_fft_s1_kernel, tn=tn, halves=halves),
        out_shape=jax.ShapeDtypeStruct((bsz, halves, r, nc, d), jnp.uint32),
        grid=(bsz, nblk),
        in_specs=x_specs + tw_specs + [
            _resident((d, d), const),
            _resident((2 * r, r), const)],
        out_specs=pl.BlockSpec((None, halves, r, tn, d), lambda b, i: (b, 0, 0, i, 0)),
        compiler_params=_cparams(2),
        name="fft_s1",
    )(*([x] * halves), *([tw] * halves), w_xb, fs1)


def _fft_s2_kernel(a_ref, x_ref, h_ref, ya_ref, mod_ref, wm_ref, m2_ref, cc_ref, cs_ref, wbo_ref, wo_ref,
                   fg_ref, o_ref, *, tk1, final):
    r = x_ref.shape[0]
    d = fg_ref.shape[1]
    gw = d // N_FOURIER_GROUPS
    _, _, gate = _mod_rows(mod_ref, d)
    cols = [slice(q * d, (q + 1) * d) for q in range(tk1)]

    us = []
    for q in range(tk1):
        re, im = _unpack_pair(a_ref[q])
        rhs = jnp.concatenate([re.astype(BF16), im.astype(BF16)], axis=0)
        us.append(jnp.dot(m2_ref[...], rhs, preferred_element_type=F32))
    ur = jnp.concatenate([u[:r] for u in us], axis=0).astype(BF16)
    uq = jnp.concatenate([u[r:] for u in us], axis=0).astype(BF16)
    x = jnp.concatenate([x_ref[:, cs] for cs in cols], axis=0)
    ya = jnp.concatenate([ya_ref[:, cs] for cs in cols], axis=0)

    hn, sgb = _unpack_pair(jnp.concatenate([h_ref[:, cs] for cs in cols], axis=0))
    gm = jnp.dot(hn.astype(BF16), wm_ref[...], preferred_element_type=F32)
    y = jnp.concatenate(
        [jnp.dot(ur[:, gi * gw:(gi + 1) * gw], cc_ref[...], preferred_element_type=F32)
         + jnp.dot(uq[:, gi * gw:(gi + 1) * gw], cs_ref[...], preferred_element_type=F32)
         for gi in range(N_FOURIER_GROUPS)], axis=1)
    yb = jnp.dot((y * sgb).astype(BF16), wbo_ref[...], preferred_element_type=F32)
    mix = _sigmoid(gm[:, :d]) * ya + _sigmoid(gm[:, d:]) * yb
    out = jnp.dot(mix.astype(BF16), wo_ref[...], preferred_element_type=F32)
    xn = x + gate * out
    if final:
        ms = jnp.mean(xn * xn, axis=-1, keepdims=True)
        xn = xn * lax.rsqrt(ms + EPS) * fg_ref[...]
    for q, cs in enumerate(cols):
        if len(o_ref.shape) == 3:
            o_ref[:, q, :] = xn[q * r:(q + 1) * r, :]
        else:
            o_ref[:, cs] = xn[q * r:(q + 1) * r, :]


def _fft_s2(a, x, h, ya, mod, w_m, m2, cc, cs, w_b_out, w_o, final_g, *, tk1, final):
    bsz, r, ld = x.shape
    d = ld // TOK
    halves = a.shape[1]
    nc = a.shape[3]
    nblk = (TOK // halves) // tk1
    const = lambda b, k2, i: (0, 0)
    tok = pl.BlockSpec((None, r, tk1 * d), lambda b, k2, i: (b, 0, k2 * nblk + i))
    gw = d // N_FOURIER_GROUPS
    if final and tk1 % SUBLANES == 0:
        out_shape = jax.ShapeDtypeStruct((bsz, r, TOK, d), F32)
        out_spec = pl.BlockSpec((None, r, tk1, d), lambda b, k2, i: (b, 0, k2 * nblk + i, 0))
    else:
        out_shape = jax.ShapeDtypeStruct((bsz, r, ld), F32)
        out_spec = tok
    return pl.pallas_call(
        functools.partial(_fft_s2_kernel, tk1=tk1, final=final),
        out_shape=out_shape,
        grid=(bsz, halves, nblk),
        in_specs=[pl.BlockSpec((None, None, tk1, nc, d), lambda b, k2, i: (b, k2, i, 0, 0)),
                  tok, tok, tok,
                  pl.BlockSpec((None, 1, 3 * d), lambda b, k2, i: (b, 0, 0)),
                  _resident((d, 2 * d), const),
                  pl.BlockSpec((None, 2 * nc, 2 * nc), lambda b, k2, i: (k2, 0, 0)),
                  _resident((gw, gw), const),
                  _resident((gw, gw), const),
                  _resident((d, d), const),
                  _resident((d, d), const),
                  _resident((1, d), const)],
        out_specs=out_spec,
        compiler_params=_cparams(3),
        name="fft_s2",
    )(a, x, h, ya, mod, w_m, m2, cc, cs, w_b_out, w_o, final_g)


def _dft_tables(r, tn, gw):
    halves = TOK // r
    s = r * TOK
    k1 = np.arange(r)
    ang1 = 2.0 * np.pi * np.outer(k1, k1) / r
    fs1 = np.concatenate([np.cos(ang1), np.sin(ang1)], axis=0)
    tok = np.arange(TOK)
    ang_t = 2.0 * np.pi * np.outer(k1, tok) / s
    scale = 1.0 / np.sqrt(float(s) * gw)
    tc = (np.cos(ang_t) * scale).reshape(r, TOK // tn, tn).transpose(1, 0, 2)
    ts = (np.sin(ang_t) * scale).reshape(r, TOK // tn, tn).transpose(1, 0, 2)
    tw = np.concatenate([tc, ts], axis=2)
    n = np.arange(r)
    m2 = []
    for k2 in range(halves):
        ang2 = 2.0 * np.pi * np.outer(halves * n + k2, n) / (halves * r)
        c2, s2 = np.cos(ang2), np.sin(ang2)
        m2.append(np.block([[c2, -s2], [s2, c2]]))
    c = np.arange(gw)
    angc = 2.0 * np.pi * np.outer(c, c) / gw
    return (jnp.asarray(fs1, BF16), jnp.asarray(tw, F32), jnp.asarray(np.stack(m2), BF16),
            jnp.asarray(np.cos(angc), BF16), jnp.asarray(-np.sin(angc), BF16))


def _trunk(x, mods, lw, final_g):
    bsz, s, d = x.shape
    r = s // TOK
    halves = TOK // r
    assert r * TOK == s and r * halves == TOK and halves in (1, 2) and r % SUBLANES == 0
    tn = 8
    tk1 = 512 // r
    gw = d // N_FOURIER_GROUPS
    fs1, tw, m2, cc, cs = _dft_tables(r, tn, gw)
    depth = len(lw)
    for l in range(depth):
        w = lw[l]
        mod = mods[l]
        res = _lru_fwd(x, mod, w["norm_g"], w["w_xa"], w["conv_w"], w["conv_b"], w["wg"][0], w["lam"][0],
                       w["w_gb"], natural_in=(l == 0))
        xc, hf, hbf, h = res[:4]
        if l == 0:
            x = res[4]
        ya = _lru_bwd(xc, hf, hbf, w["w_ga"], w["wg"][1], w["lam"][1], w["w_a_out"])
        a = _fft_s1(h, w["w_xb"], fs1, tw, tn=tn, halves=halves)
        x = _fft_s2(a, x, h, ya, mod, w["w_m"], m2, cc, cs, w["w_b_out"], w["w_o"], final_g,
                    tk1=tk1, final=(l == depth - 1))
    return x.reshape(bsz, s, d)


def _gate_weights(w, b):
    n_heads, blk = w.shape[1], w.shape[2]
    bias = 0.5 * jnp.concatenate([b[0].reshape(n_heads, 1, blk), b[1].reshape(n_heads, 1, blk)], axis=-1)
    rows = jnp.concatenate([jnp.concatenate([w[0], w[1]], axis=-1), bias,
                            jnp.zeros((n_heads, blk - 1, 2 * blk), F32)], axis=1)
    return rows.astype(BF16)


def _layer_weights(norm_g, w_in, conv_w, conv_b, w_rg, b_rg, lam, w_a_out, w_b_out, w_o):
    depth, d = norm_g.shape
    lw = []
    for l in range(depth):
        wi = w_in[l].astype(BF16)
        lw.append(dict(
            norm_g=norm_g[l].reshape(1, d),
            w_xa=wi[:, 0:d], w_ga=wi[:, d:2 * d], w_xb=wi[:, 2 * d:3 * d], w_gb=wi[:, 3 * d:4 * d],
            w_m=wi[:, 4 * d:6 * d],
            conv_w=conv_w[l], conv_b=conv_b[l].reshape(1, d),
            wg=[_gate_weights(w_rg[l, dr], b_rg[l, dr]) for dr in range(2)],
            lam=[lam[l, dr].reshape(1, d) for dr in range(2)],
            w_a_out=w_a_out[l].astype(BF16), w_b_out=w_b_out[l].astype(BF16), w_o=w_o[l].astype(BF16)))
    return lw


def kernel(x_prompt, x_sample, c_prompt, c_sample, norm_g, w_ada, b_ada, w_in, conv_w, conv_b, w_rg, b_rg, lam,
           w_a_out, w_b_out, w_o, final_g):
    d = x_prompt.shape[-1]
    depth = norm_g.shape[0]
    bp, bs = c_prompt.shape[0], c_sample.shape[0]
    rows = -(-(bp + bs) // SUBLANES) * SUBLANES
    c_all = jnp.concatenate([c_prompt, c_sample, jnp.zeros((rows - bp - bs, d), F32)], axis=0)
    mod = _adaln_mod(c_all, w_ada, b_ada)
    mods_p = [mod[l, 0:bp].reshape(bp, 1, 3 * d) for l in range(depth)]
    mods_s = [mod[l, bp:bp + bs].reshape(bs, 1, 3 * d) for l in range(depth)]
    lw = _layer_weights(norm_g, w_in, conv_w, conv_b, w_rg, b_rg, lam, w_a_out, w_b_out, w_o)
    fg = final_g.reshape(1, d)
    return (_trunk(x_prompt, mods_p, lw, fg), _trunk(x_sample, mods_s, lw, fg))
```

```python
import functools

import numpy as np
import jax
import jax.numpy as jnp
from jax import lax
from jax.experimental import pallas as pl
from jax.experimental.pallas import tpu as pltpu

F32 = jnp.float32
BF16 = jnp.bfloat16

N_LRU_HEADS = 8
N_FOURIER_GROUPS = 4
CONV_WIDTH = 4
CONV_LEFT = 2
LRU_C = 8.0
EPS = 1e-6

SUBLANES = 8
TOK = 128
ROW_CHUNK = 256
XA_COLS = 256
XB_COLS = 256
VMEM_LIMIT = 58 * 1024 * 1024
TINY = 1e-30
LOG2E = 1.4426950408889634


def _cparams(n_axes):
    return pltpu.CompilerParams(dimension_semantics=("arbitrary",) * n_axes,
                                vmem_limit_bytes=VMEM_LIMIT)


def _resident(shape, index_map):
    return pl.BlockSpec(shape, index_map, pipeline_mode=pl.Buffered(1))


def _sigmoid(z):
    return 0.5 * jnp.tanh(0.5 * z) + 0.5


def _silu(z):
    zh = 0.5 * z
    return zh * (jnp.tanh(zh) + 1.0)


def _softplus(z):
    return jnp.maximum(z, 0.0) + jnp.log1p(jnp.exp(-jnp.abs(z)))


def _norm_mod(x, g1s, shift):
    ms = jnp.mean(x * x, axis=-1, keepdims=True)
    return x * lax.rsqrt(ms + EPS) * g1s + shift


def _pack_pair(hi, lo):
    h = lax.bitcast_convert_type(hi.astype(BF16).astype(F32), jnp.uint32)
    l = lax.bitcast_convert_type(lo.astype(BF16).astype(F32), jnp.uint32)
    return h | (l >> 16)


def _unpack_pair(w):
    return (lax.bitcast_convert_type(w & jnp.uint32(0xFFFF0000), F32),
            lax.bitcast_convert_type(w << 16, F32))


def _mod_rows(mod_ref, d):
    return mod_ref[:, 0:d], mod_ref[:, d:2 * d], mod_ref[:, 2 * d:3 * d]


def _norm_to_rows(x_ref, hbf_ref, g1s, shift, n_groups, xs_ref=None):
    d = g1s.shape[1]
    g = SUBLANES

    def group(l):
        if xs_ref is None:
            return x_ref[:, l * d:(l + 1) * d]
        xg = x_ref[:, l, :]
        xs_ref[:, l * d:(l + 1) * d] = xg
        return xg

    for l in range(0, n_groups, 2):
        xx = jnp.concatenate([group(l), group(l + 1)], axis=0)
        hbf_ref[l * g:(l + 2) * g, :] = _norm_mod(xx, g1s, shift).astype(BF16)


def _mod_kernel(c_ref, w_ref, b_ref, o_ref):
    c = c_ref[...]
    o_ref[...] = jnp.dot(_silu(c), w_ref[...], preferred_element_type=F32,
                         precision=lax.Precision.HIGHEST) + b_ref[...]


def _adaln_mod(c_all, w_ada, b_ada):
    depth, d, d3 = w_ada.shape
    rows = c_all.shape[0]
    return pl.pallas_call(
        _mod_kernel,
        out_shape=jax.ShapeDtypeStruct((depth, rows, d3), F32),
        grid=(depth, d3 // d),
        in_specs=[pl.BlockSpec((rows, d), lambda l, j: (0, 0)),
                  pl.BlockSpec((None, d, d), lambda l, j: (l, 0, j)),
                  pl.BlockSpec((None, 1, d), lambda l, j: (l, 0, j))],
        out_specs=pl.BlockSpec((None, rows, d), lambda l, j: (l, 0, j)),
        compiler_params=_cparams(2),
        name="adaln_mod",
    )(c_all, w_ada, b_ada.reshape(depth, 1, d3))


def _lru_coeffs(xh, gh, k2):
    hb = xh.shape[1]
    t_r = jnp.tanh(gh[:, :hb])
    t_i = jnp.tanh(gh[:, hb:])
    a = jnp.exp2((t_r + 1.0) * k2)
    y = 1.0 - a * a
    mult = y * lax.rsqrt(jnp.maximum(y, TINY))
    u = (t_i + 1.0) * xh
    return a, mult * u, u


def _decay_const(lam_ref):
    return (-0.5 * LRU_C * LOG2E) * _softplus(-lam_ref[...])


def _gate_dot(xhb, ones, wg):
    return jnp.dot(jnp.concatenate([xhb, ones], axis=1), wg, preferred_element_type=F32)


def _ones_column(n_rows, width):
    lane = lax.broadcasted_iota(jnp.int32, (n_rows, width), 1)
    return jnp.where(lane == 0, 1.0, 0.0).astype(BF16)


def _segment_ends(a_ref, b_ref, reverse):
    g = SUBLANES
    n_groups = a_ref.shape[0] // g
    d = a_ref.shape[1]

    def body(s, hp):
        l = (n_groups - 1 - s) if reverse else s
        r0 = pl.multiple_of(l * g, g)
        a = a_ref[pl.ds(r0, g), :]
        return (a * hp[0] + b_ref[pl.ds(r0, g), :], a * hp[1])

    return lax.fori_loop(0, n_groups, body, (jnp.zeros((g, d), F32), jnp.ones((g, d), F32)), unroll=8)


def _segment_apply(a_ref, b_ref, c_in, out_ref, reverse):
    g = SUBLANES
    n_pairs = a_ref.shape[0] // (2 * g)

    def body(s, h):
        p = (n_pairs - 1 - s) if reverse else s
        r0 = pl.multiple_of(p * 2 * g, 2 * g)
        a = a_ref[pl.ds(r0, 2 * g), :]
        b = b_ref[pl.ds(r0, 2 * g), :]
        if reverse:
            h1 = a[g:] * h + b[g:]
            h0 = a[:g] * h1 + b[:g]
            h = h0
        else:
            h0 = a[:g] * h + b[:g]
            h1 = a[g:] * h0 + b[g:]
            h = h1
        out_ref[pl.ds(r0, 2 * g), :] = jnp.concatenate([h0, h1], axis=0).astype(out_ref.dtype)
        return h

    lax.fori_loop(0, n_pairs, body, c_in, unroll=4)


def _segment_apply_rows(a_ref, b_ref, h, out_ref, r0, r1, reverse):
    g = SUBLANES
    starts = range(r0, r1, 2 * g)
    for p0 in (reversed(starts) if reverse else starts):
        a = a_ref[p0:p0 + 2 * g, :]
        b = b_ref[p0:p0 + 2 * g, :]
        if reverse:
            h1 = a[g:] * h + b[g:]
            h0 = a[:g] * h1 + b[:g]
            h = h0
        else:
            h0 = a[:g] * h + b[:g]
            h1 = a[g:] * h0 + b[g:]
            h = h1
        out_ref[p0:p0 + 2 * g, :] = jnp.concatenate([h0, h1], axis=0).astype(out_ref.dtype)
    return h


def _segment_carries(h_end, p_end, carry, reverse):
    rows = [None] * SUBLANES
    c = carry
    for s in (range(SUBLANES - 1, -1, -1) if reverse else range(SUBLANES)):
        rows[s] = c
        c = h_end[s:s + 1, :] + p_end[s:s + 1, :] * c
    return jnp.concatenate(rows, axis=0), c


def _lru_fwd_kernel(*refs, n_tiles, natural_in):
    (x_ref, xn_ref, mod_ref, g_ref, wxa_ref, cw_ref, cb_ref, wg_ref, lam_ref, wgb_ref,
     xc_ref, hf_ref, hbf_ref, hs_ref) = refs[:14]
    xs_ref = refs[14] if natural_in else None
    ext_ref, a_ref, b_ref, carry_ref, prev_ref, u_ref = refs[-6:]
    i = pl.program_id(1)
    g = SUBLANES
    d = g_ref.shape[1]
    n_rows = a_ref.shape[0]
    n_groups = n_rows // g
    hb = d // N_LRU_HEADS
    shift, scale, _ = _mod_rows(mod_ref, d)
    g1s = g_ref[...] * (1.0 + scale)

    @pl.when(i == 0)
    def _():
        carry_ref[...] = jnp.zeros_like(carry_ref)
        prev_ref[...] = jnp.zeros_like(prev_ref)

    _norm_to_rows(x_ref, hbf_ref, g1s, shift, n_groups, xs_ref)
    xn = xn_ref[:, 0, :] if natural_in else xn_ref[...]
    hbf_ref[n_rows:n_rows + 2 * g, :] = _norm_mod(jnp.concatenate([xn, xn], axis=0), g1s, shift).astype(BF16)
    row = lax.broadcasted_iota(jnp.int32, (g, XA_COLS), 0)
    nxt_scale = jnp.where(i == n_tiles - 1, 0.0, 1.0)

    def xa_block(k):
        cs = slice(k * XA_COLS, (k + 1) * XA_COLS)
        ext_ref[CONV_LEFT * g:, cs] = jnp.dot(hbf_ref[...], wxa_ref[:, cs], preferred_element_type=F32)
        last1 = ext_ref[(n_groups + 1) * g:(n_groups + 2) * g, cs]
        last2 = ext_ref[n_groups * g:(n_groups + 1) * g, cs]
        first = ext_ref[CONV_LEFT * g:(CONV_LEFT + 1) * g, cs]
        nxt = ext_ref[(n_groups + 2) * g:(n_groups + 3) * g, cs]
        ext_ref[g:2 * g, cs] = jnp.where(row == 0, pltpu.roll(prev_ref[g:2 * g, cs], 1, axis=0),
                                         pltpu.roll(last1, 1, axis=0))
        ext_ref[0:g, cs] = jnp.where(row == 0, pltpu.roll(prev_ref[0:g, cs], 1, axis=0),
                                     pltpu.roll(last2, 1, axis=0))
        prev_ref[0:g, cs] = last2
        prev_ref[g:2 * g, cs] = last1
        ext_ref[(n_groups + 2) * g:(n_groups + 3) * g, cs] = jnp.where(
            row == g - 1, pltpu.roll(nxt, g - 1, axis=0) * nxt_scale, pltpu.roll(first, g - 1, axis=0))

    k2 = _decay_const(lam_ref)
    ones = _ones_column(n_rows, hb)
    hcw = 0.5 * cw_ref[...]
    hcb = 0.5 * cb_ref[...]
    heads_per_block = XA_COLS // hb
    xa_block(0)
    for h in range(N_LRU_HEADS):
        if h % heads_per_block == 0:
            if (h // heads_per_block + 1) * XA_COLS < d:
                xa_block(h // heads_per_block + 1)
            c0 = h * hb
            sgb = _silu(jnp.dot(hbf_ref[0:n_rows, :], wgb_ref[:, c0:c0 + XA_COLS], preferred_element_type=F32))
            w = _pack_pair(hbf_ref[0:n_rows, c0:c0 + XA_COLS].astype(F32), sgb)
            for l in range(n_groups):
                hs_ref[:, l * d + c0:l * d + c0 + XA_COLS] = w[l * g:(l + 1) * g, :]
        sl = slice(h * hb, (h + 1) * hb)
        xh = ext_ref[0:n_rows, sl] * hcw[0:1, sl] + hcb[:, sl]
        for kk in range(1, CONV_WIDTH):
            xh = xh + ext_ref[kk * g:kk * g + n_rows, sl] * hcw[kk:kk + 1, sl]
        xhb = xh.astype(BF16)
        xc_ref[:, sl] = xhb
        a, b, u = _lru_coeffs(xh, _gate_dot(xhb, ones, wg_ref[h]), k2[:, sl])
        a_ref[:, sl] = a
        b_ref[:, sl] = b
        u_ref[:, sl] = u[0:g, :]

    @pl.when(i == 0)
    def _():
        first_row = lax.broadcasted_iota(jnp.int32, (g, d), 0) == 0
        b_ref[0:g, :] = jnp.where(first_row, u_ref[...], b_ref[0:g, :])

    h_end, p_end = _segment_ends(a_ref, b_ref, False)
    c_in, carry_ref[...] = _segment_carries(h_end, p_end, carry_ref[...], False)
    _segment_apply(a_ref, b_ref, c_in, hf_ref, False)


def _lru_fwd(x, mod, norm_g, w_xa, conv_w, conv_b, wg, lam, w_gb, *, natural_in):
    g = SUBLANES
    if natural_in:
        bsz, s, d = x.shape
        r = s // TOK
        x = x.reshape(bsz, r, TOK, d)
        x_spec = pl.BlockSpec((None, g, TOK, d), lambda b, i: (b, i, 0, 0))
        xn_spec = pl.BlockSpec((None, g, g, d), lambda b, i: (b, jnp.minimum(i + 1, n_tiles - 1), 0, 0))
    else:
        bsz, r, ld = x.shape
        d = ld // TOK
        x_spec = pl.BlockSpec((None, g, TOK * d), lambda b, i: (b, i, 0))
        xn_spec = pl.BlockSpec((None, g, d), lambda b, i: (b, jnp.minimum(i + 1, n_tiles - 1), 0))
    n_tiles = r // g
    n_rows = TOK * g
    hb = d // N_LRU_HEADS
    const = lambda b, i: (0, 0)
    rowform = jax.ShapeDtypeStruct((bsz, n_tiles, n_rows, d), BF16)
    rowspec = pl.BlockSpec((None, None, n_rows, d), lambda b, i: (b, i, 0, 0))
    storage = jax.ShapeDtypeStruct((bsz, r, TOK * d), F32)
    storage_spec = pl.BlockSpec((None, g, TOK * d), lambda b, i: (b, i, 0))
    out_shape = [rowform, rowform, jax.ShapeDtypeStruct((bsz, n_tiles, n_rows + 2 * g, d), BF16),
                 jax.ShapeDtypeStruct((bsz, r, TOK * d), jnp.uint32)]
    out_specs = [rowspec, rowspec, pl.BlockSpec((None, None, n_rows + 2 * g, d), lambda b, i: (b, i, 0, 0)),
                 storage_spec]
    if natural_in:
        out_shape.append(storage)
        out_specs.append(storage_spec)
    return pl.pallas_call(
        functools.partial(_lru_fwd_kernel, n_tiles=n_tiles, natural_in=natural_in),
        out_shape=tuple(out_shape),
        grid=(bsz, n_tiles),
        in_specs=[x_spec, xn_spec,
                  pl.BlockSpec((None, 1, 3 * d), lambda b, i: (b, 0, 0)),
                  _resident((1, d), const),
                  _resident((d, d), const),
                  _resident((CONV_WIDTH, d), const),
                  _resident((1, d), const),
                  _resident((N_LRU_HEADS, 2 * hb, 2 * hb), lambda b, i: (0, 0, 0)),
                  _resident((1, d), const),
                  _resident((d, d), const)],
        out_specs=tuple(out_specs),
        scratch_shapes=[pltpu.VMEM((n_rows + (CONV_LEFT + 2) * g, d), F32),
                        pltpu.VMEM((n_rows, d), F32),
                        pltpu.VMEM((n_rows, d), F32),
                        pltpu.VMEM((1, d), F32),
                        pltpu.VMEM((CONV_LEFT * g, d), F32),
                        pltpu.VMEM((g, d), F32)],
        compiler_params=_cparams(2),
        name="lru_fwd",
    )(x, x, mod, norm_g, w_xa, conv_w, conv_b, wg, lam, w_gb)


def _lru_bwd_kernel(xc_ref, hf_ref, hbf_ref, wga_ref, wg_ref, lam_ref, wao_ref,
                    ya_ref, a_ref, b_ref, sg_ref, carry_ref, u_ref):
    i = pl.program_id(1)
    g = SUBLANES
    n_rows, d = a_ref.shape
    hb = d // N_LRU_HEADS

    @pl.when(i == 0)
    def _():
        carry_ref[...] = jnp.zeros_like(carry_ref)

    k2 = _decay_const(lam_ref)
    ones = _ones_column(n_rows, hb)
    heads_per_block = XA_COLS // hb
    for h in range(N_LRU_HEADS):
        if h % heads_per_block == 0:
            cs = slice(h * hb, h * hb + XA_COLS)
            sg_ref[:, cs] = _silu(jnp.dot(hbf_ref[0:n_rows, :], wga_ref[:, cs], preferred_element_type=F32))
        sl = slice(h * hb, (h + 1) * hb)
        xhb = xc_ref[:, sl]
        a, b, u = _lru_coeffs(xhb.astype(F32), _gate_dot(xhb, ones, wg_ref[h]), k2[:, sl])
        a_ref[:, sl] = a
        b_ref[:, sl] = b
        u_ref[:, sl] = u[n_rows - g:, :]

    @pl.when(i == 0)
    def _():
        row = lax.broadcasted_iota(jnp.int32, (g, d), 0)
        b_ref[n_rows - g:, :] = jnp.where(row == g - 1, u_ref[...], b_ref[n_rows - g:, :])

    h_end, p_end = _segment_ends(a_ref, b_ref, True)
    c_in, carry_ref[...] = _segment_carries(h_end, p_end, carry_ref[...], True)
    h = c_in
    for r0 in range(n_rows - ROW_CHUNK, -1, -ROW_CHUNK):
        rs = slice(r0, r0 + ROW_CHUNK)
        h = _segment_apply_rows(a_ref, b_ref, h, b_ref, r0, r0 + ROW_CHUNK, True)
        ya = (hf_ref[rs, :].astype(F32) + b_ref[rs, :]) * sg_ref[rs, :]
        out = jnp.dot(ya.astype(BF16), wao_ref[...], preferred_element_type=F32)
        for gl in range(ROW_CHUNK // g):
            l = r0 // g + gl
            ya_ref[:, l * d:(l + 1) * d] = out[gl * g:(gl + 1) * g, :]


def _lru_bwd(xc, hf, hbf, w_ga, wg, lam, w_a_out):
    bsz, n_tiles, n_rows, d = xc.shape
    g = SUBLANES
    hb = d // N_LRU_HEADS
    const = lambda b, i: (0, 0)
    rev3 = lambda b, i: (b, n_tiles - 1 - i, 0)
    rev4 = lambda b, i: (b, n_tiles - 1 - i, 0, 0)
    return pl.pallas_call(
        _lru_bwd_kernel,
        out_shape=jax.ShapeDtypeStruct((bsz, n_tiles * g, TOK * d), F32),
        grid=(bsz, n_tiles),
        in_specs=[pl.BlockSpec((None, None, n_rows, d), rev4),
                  pl.BlockSpec((None, None, n_rows, d), rev4),
                  pl.BlockSpec((None, None, n_rows + 2 * g, d), rev4),
                  _resident((d, d), const),
                  _resident((N_LRU_HEADS, 2 * hb, 2 * hb), lambda b, i: (0, 0, 0)),
                  _resident((1, d), const),
                  _resident((d, d), const)],
        out_specs=pl.BlockSpec((None, g, TOK * d), rev3),
        scratch_shapes=[pltpu.VMEM((n_rows, d), F32),
                        pltpu.VMEM((n_rows, d), F32),
                        pltpu.VMEM((n_rows, d), F32),
                        pltpu.VMEM((1, d), F32),
                        pltpu.VMEM((g, d), F32)],
        compiler_params=_cparams(2),
        name="lru_bwd",
    )(xc, hf, hbf, w_ga, wg, lam, w_a_out)


def _fft_s1_kernel(*refs, tn, halves):
    x_refs = refs[:halves]
    tw_refs = refs[halves:2 * halves]
    wxb_ref, fs1_ref, o_ref = refs[2 * halves:]
    r = x_refs[0].shape[0]
    d = wxb_ref.shape[0]
    hn_all = jnp.concatenate(
        [_unpack_pair(x_refs[hf][:, j * d:(j + 1) * d])[0].astype(BF16)
         for hf in range(halves) for j in range(tn)], axis=0)
    def xb_block(k):
        return jnp.dot(hn_all, wxb_ref[:, k * XB_COLS:(k + 1) * XB_COLS], preferred_element_type=F32).astype(BF16)

    xb_next = xb_block(0)
    for k in range(d // XB_COLS):
        cs = slice(k * XB_COLS, (k + 1) * XB_COLS)
        xb, xb_next = xb_next, (xb_block(k + 1) if (k + 1) * XB_COLS < d else None)
        words = [[] for _ in range(halves)]
        for j in range(tn):
            parts = []
            for hf in range(halves):
                c0 = (hf * tn + j) * r
                pq = jnp.dot(fs1_ref[...], xb[c0:c0 + r, :], preferred_element_type=F32)
                p, q = pq[:r], pq[r:]
                tc = tw_refs[hf][:, j:j + 1]
                ts = tw_refs[hf][:, tn + j:tn + j + 1]
                parts.append((p * tc - q * ts, p * ts + q * tc))
            if halves == 1:
                words[0].append(_pack_pair(*parts[0]))
            else:
                words[0].append(_pack_pair(parts[0][0] + parts[1][0], parts[0][1] + parts[1][1]))
                words[1].append(_pack_pair(parts[0][0] - parts[1][0], parts[0][1] - parts[1][1]))
        for k2 in range(halves):
            o_ref[k2, :, :, cs] = jnp.swapaxes(jnp.stack(words[k2], axis=0), 0, 1)


def _fft_s1(x, w_xb, fs1, tw, *, tn, halves):
    bsz, r, ld = x.shape
    d = ld // TOK
    nc = TOK // halves
    nblk = nc // tn
    const = lambda b, i: (0, 0)
    x_specs = [pl.BlockSpec((None, r, tn * d), functools.partial(lambda b, i, hf: (b, 0, hf * nblk + i), hf=hf))
               for hf in range(halves)]
    tw_specs = [pl.BlockSpec((None, r, 2 * tn), functools.partial(lambda b, i, hf: (hf * nblk + i, 0, 0), hf=hf))
                for hf in range(halves)]
    return pl.pallas_call(
        functools.partial(_fft_s1_kernel, tn=tn, halves=halves),
        out_shape=jax.ShapeDtypeStruct((bsz, halves, r, nc, d), jnp.uint32),
        grid=(bsz, nblk),
        in_specs=x_specs + tw_specs + [
            _resident((d, d), const),
            _resident((2 * r, r), const)],
        out_specs=pl.BlockSpec((None, halves, r, tn, d), lambda b, i: (b, 0, 0, i, 0)),
        compiler_params=_cparams(2),
        name="fft_s1",
    )(*([x] * halves), *([tw] * halves), w_xb, fs1)


def _fft_s2_kernel(a_ref, x_ref, h_ref, ya_ref, mod_ref, wm_ref, m2_ref, cc_ref, cs_ref, wbo_ref, wo_ref,
                   fg_ref, o_ref, *, tk1, final):
    r = x_ref.shape[0]
    d = fg_ref.shape[1]
    gw = d // N_FOURIER_GROUPS
    _, _, gate = _mod_rows(mod_ref, d)
    cols = [slice(q * d, (q + 1) * d) for q in range(tk1)]

    us = []
    for q in range(tk1):
        re, im = _unpack_pair(a_ref[q])
        rhs = jnp.concatenate([re.astype(BF16), im.astype(BF16)], axis=0)
        us.append(jnp.dot(m2_ref[...], rhs, preferred_element_type=F32))
    ur = jnp.concatenate([u[:r] for u in us], axis=0).astype(BF16)
    uq = jnp.concatenate([u[r:] for u in us], axis=0).astype(BF16)
    x = jnp.concatenate([x_ref[:, cs] for cs in cols], axis=0)
    ya = jnp.concatenate([ya_ref[:, cs] for cs in cols], axis=0)

    hn, sgb = _unpack_pair(jnp.concatenate([h_ref[:, cs] for cs in cols], axis=0))
    gm = jnp.dot(hn.astype(BF16), wm_ref[...], preferred_element_type=F32)
    y = jnp.concatenate(
        [jnp.dot(ur[:, gi * gw:(gi + 1) * gw], cc_ref[...], preferred_element_type=F32)
         + jnp.dot(uq[:, gi * gw:(gi + 1) * gw], cs_ref[...], preferred_element_type=F32)
         for gi in range(N_FOURIER_GROUPS)], axis=1)
    yb = jnp.dot((y * sgb).astype(BF16), wbo_ref[...], preferred_element_type=F32)
    mix = _sigmoid(gm[:, :d]) * ya + _sigmoid(gm[:, d:]) * yb
    out = jnp.dot(mix.astype(BF16), wo_ref[...], preferred_element_type=F32)
    xn = x + gate * out
    if final:
        ms = jnp.mean(xn * xn, axis=-1, keepdims=True)
        xn = xn * lax.rsqrt(ms + EPS) * fg_ref[...]
    if len(o_ref.shape) == 3:
        o_ref[...] = jnp.swapaxes(xn.reshape(tk1, r, d), 0, 1)
    else:
        for q, cs in enumerate(cols):
            o_ref[:, cs] = xn[q * r:(q + 1) * r, :]


def _fft_s2(a, x, h, ya, mod, w_m, m2, cc, cs, w_b_out, w_o, final_g, *, tk1, final):
    bsz, r, ld = x.shape
    d = ld // TOK
    halves = a.shape[1]
    nc = a.shape[3]
    nblk = (TOK // halves) // tk1
    const = lambda b, k2, i: (0, 0)
    tok = pl.BlockSpec((None, r, tk1 * d), lambda b, k2, i: (b, 0, k2 * nblk + i))
    gw = d // N_FOURIER_GROUPS
    if final and tk1 % SUBLANES == 0:
        out_shape = jax.ShapeDtypeStruct((bsz, r, TOK, d), F32)
        out_spec = pl.BlockSpec((None, r, tk1, d), lambda b, k2, i: (b, 0, k2 * nblk + i, 0))
    else:
        out_shape = jax.ShapeDtypeStruct((bsz, r, ld), F32)
        out_spec = tok
    return pl.pallas_call(
        functools.partial(_fft_s2_kernel, tk1=tk1, final=final),
        out_shape=out_shape,
        grid=(bsz, halves, nblk),
        in_specs=[pl.BlockSpec((None, None, tk1, nc, d), lambda b, k2, i: (b, k2, i, 0, 0)),
                  tok, tok, tok,
                  pl.BlockSpec((None, 1, 3 * d), lambda b, k2, i: (b, 0, 0)),
                  _resident((d, 2 * d), const),
                  pl.BlockSpec((None, 2 * nc, 2 * nc), lambda b, k2, i: (k2, 0, 0)),
                  _resident((gw, gw), const),
                  _resident((gw, gw), const),
                  _resident((d, d), const),
                  _resident((d, d), const),
                  _resident((1, d), const)],
        out_specs=out_spec,
        compiler_params=_cparams(3),
        name="fft_s2",
    )(a, x, h, ya, mod, w_m, m2, cc, cs, w_b_out, w_o, final_g)


def _dft_tables(r, tn, gw):
    halves = TOK // r
    s = r * TOK
    k1 = np.arange(r)
    ang1 = 2.0 * np.pi * np.outer(k1, k1) / r
    fs1 = np.concatenate([np.cos(ang1), np.sin(ang1)], axis=0)
    tok = np.arange(TOK)
    ang_t = 2.0 * np.pi * np.outer(k1, tok) / s
    scale = 1.0 / np.sqrt(float(s) * gw)
    tc = (np.cos(ang_t) * scale).reshape(r, TOK // tn, tn).transpose(1, 0, 2)
    ts = (np.sin(ang_t) * scale).reshape(r, TOK // tn, tn).transpose(1, 0, 2)
    tw = np.concatenate([tc, ts], axis=2)
    n = np.arange(r)
    m2 = []
    for k2 in range(halves):
        ang2 = 2.0 * np.pi * np.outer(halves * n + k2, n) / (halves * r)
        c2, s2 = np.cos(ang2), np.sin(ang2)
        m2.append(np.block([[c2, -s2], [s2, c2]]))
    c = np.arange(gw)
    angc = 2.0 * np.pi * np.outer(c, c) / gw
    return (jnp.asarray(fs1, BF16), jnp.asarray(tw, F32), jnp.asarray(np.stack(m2), BF16),
            jnp.asarray(np.cos(angc), BF16), jnp.asarray(-np.sin(angc), BF16))


def _trunk(x, mods, lw, final_g):
    bsz, s, d = x.shape
    r = s // TOK
    halves = TOK // r
    assert r * TOK == s and r * halves == TOK and halves in (1, 2) and r % SUBLANES == 0
    tn = 8
    tk1 = 512 // r
    gw = d // N_FOURIER_GROUPS
    fs1, tw, m2, cc, cs = _dft_tables(r, tn, gw)
    depth = len(lw)
    for l in range(depth):
        w = lw[l]
        mod = mods[l]
        res = _lru_fwd(x, mod, w["norm_g"], w["w_xa"], w["conv_w"], w["conv_b"], w["wg"][0], w["lam"][0],
                       w["w_gb"], natural_in=(l == 0))
        xc, hf, hbf, h = res[:4]
        if l == 0:
            x = res[4]
        ya = _lru_bwd(xc, hf, hbf, w["w_ga"], w["wg"][1], w["lam"][1], w["w_a_out"])
        a = _fft_s1(h, w["w_xb"], fs1, tw, tn=tn, halves=halves)
        x = _fft_s2(a, x, h, ya, mod, w["w_m"], m2, cc, cs, w["w_b_out"], w["w_o"], final_g,
                    tk1=tk1, final=(l == depth - 1))
    return x.reshape(bsz, s, d)


def _gate_weights(w, b):
    n_heads, blk = w.shape[1], w.shape[2]
    bias = 0.5 * jnp.concatenate([b[0].reshape(n_heads, 1, blk), b[1].reshape(n_heads, 1, blk)], axis=-1)
    rows = jnp.concatenate([jnp.concatenate([w[0], w[1]], axis=-1), bias,
                            jnp.zeros((n_heads, blk - 1, 2 * blk), F32)], axis=1)
    return rows.astype(BF16)


def _layer_weights(norm_g, w_in, conv_w, conv_b, w_rg, b_rg, lam, w_a_out, w_b_out, w_o):
    depth, d = norm_g.shape
    lw = []
    for l in range(depth):
        wi = w_in[l].astype(BF16)
        lw.append(dict(
            norm_g=norm_g[l].reshape(1, d),
            w_xa=wi[:, 0:d], w_ga=wi[:, d:2 * d], w_xb=wi[:, 2 * d:3 * d], w_gb=wi[:, 3 * d:4 * d],
            w_m=wi[:, 4 * d:6 * d],
            conv_w=conv_w[l], conv_b=conv_b[l].reshape(1, d),
            wg=[_gate_weights(w_rg[l, dr], b_rg[l, dr]) for dr in range(2)],
            lam=[lam[l, dr].reshape(1, d) for dr in range(2)],
            w_a_out=w_a_out[l].astype(BF16), w_b_out=w_b_out[l].astype(BF16), w_o=w_o[l].astype(BF16)))
    return lw


def kernel(x_prompt, x_sample, c_prompt, c_sample, norm_g, w_ada, b_ada, w_in, conv_w, conv_b, w_rg, b_rg, lam,
           w_a_out, w_b_out, w_o, final_g):
    d = x_prompt.shape[-1]
    depth = norm_g.shape[0]
    bp, bs = c_prompt.shape[0], c_sample.shape[0]
    rows = -(-(bp + bs) // SUBLANES) * SUBLANES
    c_all = jnp.concatenate([c_prompt, c_sample, jnp.zeros((rows - bp - bs, d), F32)], axis=0)
    mod = _adaln_mod(c_all, w_ada, b_ada)
    mods_p = [mod[l, 0:bp].reshape(bp, 1, 3 * d) for l in range(depth)]
    mods_s = [mod[l, bp:bp + bs].reshape(bs, 1, 3 * d) for l in range(depth)]
    lw = _layer_weights(norm_g, w_in, conv_w, conv_b, w_rg, b_rg, lam, w_a_out, w_b_out, w_o)
    fg = final_g.reshape(1, d)
    return (_trunk(x_prompt, mods_p, lw, fg), _trunk(x_sample, mods_s, lw, fg))
```
